```python
import jax, jax.numpy as jnp
from jax import lax
import numpy as np

D_MODEL = 1024
BATCH = 4
SEQ = 4096
DEPTH = 4
DEC_BATCH = 128
DEC_SEQ = 1
PAST_LEN = 2048
PAGE_SIZE = 128

N_HEADS = 8
HEAD_DIM = 64
D_ATTN = N_HEADS * HEAD_DIM
D_CONV = D_MODEL // 2
CONV_WIDTH = 31
D_FF = 2816
FFN_CONV_WIDTH = 3
Q_BLOCK = 128
EPS = 1e-6
N_IN = 2 * D_CONV + 3 * D_ATTN + N_HEADS + 2 * D_MODEL
SPLITS = [D_CONV, 2 * D_CONV, 2 * D_CONV + D_ATTN, 2 * D_CONV + 2 * D_ATTN, 2 * D_CONV + 3 * D_ATTN,
          2 * D_CONV + 3 * D_ATTN + N_HEADS, 2 * D_CONV + 3 * D_ATTN + N_HEADS + D_MODEL]

kernel_name = "fox_conformer_hybrid_decode_step"


def _rmsnorm(x, g):
    xf = x.astype(jnp.float32)
    y = xf * lax.rsqrt(jnp.mean(xf * xf, axis=-1, keepdims=True) + EPS)
    return (y * g.astype(jnp.float32)).astype(x.dtype)


def _layernorm(x, g, b):
    xf = x.astype(jnp.float32)
    mu = jnp.mean(xf, axis=-1, keepdims=True)
    var = jnp.mean(jnp.square(xf - mu), axis=-1, keepdims=True)
    y = (xf - mu) * lax.rsqrt(var + EPS)
    return (y * g.astype(jnp.float32) + b.astype(jnp.float32)).astype(x.dtype)


def _causal_dwconv(x, buf, w, b):
    width = w.shape[0]
    xp = jnp.concatenate([buf.astype(x.dtype), x], axis=1)
    y = lax.conv_general_dilated(xp, w[:, None, :].astype(x.dtype), window_strides=(1,), padding='VALID',
                                 dimension_numbers=('NWC', 'WIO', 'NWC'), feature_group_count=x.shape[-1])
    return y + b, xp[:, xp.shape[1] - (width - 1):]


def _in_proj(xn, w_in, b_f, q_g, k_g):
    B, T, _ = xn.shape
    h = xn @ w_in
    a_val, a_gate, q, k, v, f_logit, g_a, g_b = jnp.split(h, SPLITS, axis=-1)
    q = _rmsnorm(q.reshape(B, T, N_HEADS, HEAD_DIM), q_g)
    k = _rmsnorm(k.reshape(B, T, N_HEADS, HEAD_DIM), k_g)
    v = v.reshape(B, T, N_HEADS, HEAD_DIM)
    logf = jax.nn.log_sigmoid((f_logit + b_f).astype(jnp.float32))
    return a_val, a_gate, q, k, v, logf, g_a, g_b


def _conformer_branch(a_val, a_gate, buf, conv_w, conv_b, ln_g, ln_b, w_out):
    u = a_val * jax.nn.sigmoid(a_gate)
    c, new_buf = _causal_dwconv(u, buf, conv_w, conv_b)
    c = jax.nn.silu(_layernorm(c, ln_g, ln_b))
    return c @ w_out, new_buf


def _fox_attend(q, cq, qpos, k, v, ck, kpos):
    s = jnp.einsum('bqhd,bkhd->bhqk', q.astype(jnp.float32), k.astype(jnp.float32)) * (HEAD_DIM ** -0.5)
    s = s + jnp.transpose(cq, (0, 2, 1))[..., :, None] - jnp.transpose(ck, (0, 2, 1))[..., None, :]
    mask = kpos[None, :] <= qpos[:, None]
    s = jnp.where(mask[None, None], s, -jnp.inf)
    p = jax.nn.softmax(s, axis=-1)
    o = jnp.einsum('bhqk,bkhd->bqhd', p, v.astype(jnp.float32))
    return o.astype(q.dtype)


def _merge(g_a, a_out, g_b, b_out, w_o):
    return (jax.nn.sigmoid(g_a) * a_out + jax.nn.sigmoid(g_b) * b_out) @ w_o


def _conv_ffn(xn, buf, w_up, conv_w, conv_b, w_down):
    g, u = jnp.split(xn @ w_up, 2, axis=-1)
    gc, new_buf = _causal_dwconv(g, buf, conv_w, conv_b)
    return (jax.nn.silu(gc) * u) @ w_down, new_buf


def setup_inputs(seed: int = 0) -> dict:
    key = jax.random.key(seed)
    ks = jax.random.split(key, 32)
    f32 = jnp.float32
    n_pages = PAST_LEN // PAGE_SIZE
    n_phys = (DEC_BATCH * n_pages * 5) // 4

    def w(k, shape, fan_in):
        return jax.random.normal(k, shape, f32) * (fan_in ** -0.5)

    def gain(k, shape):
        return 1.0 + 0.02 * jax.random.normal(k, shape, f32)

    def small(k, shape):
        return 0.02 * jax.random.normal(k, shape, f32)

    page_table = jax.random.permutation(ks[0], n_phys)[:DEC_BATCH * n_pages].reshape(DEC_BATCH, n_pages).astype(jnp.int32)
    return {
        "x_prompt": jax.random.normal(ks[1], (BATCH, SEQ, D_MODEL), f32),
        "x_sample": jax.random.normal(ks[2], (DEC_BATCH, DEC_SEQ, D_MODEL), f32),
        "cache_k": jax.random.normal(ks[3], (DEPTH, n_phys, PAGE_SIZE, N_HEADS, HEAD_DIM), f32),
        "cache_v": jax.random.normal(ks[4], (DEPTH, n_phys, PAGE_SIZE, N_HEADS, HEAD_DIM), f32),
        "cache_logf": jax.nn.log_sigmoid(2.0 + 0.5 * jax.random.normal(ks[5], (DEPTH, n_phys, PAGE_SIZE, N_HEADS), f32)),
        "state_conv_a": 0.5 * jax.random.normal(ks[6], (DEPTH, DEC_BATCH, CONV_WIDTH - 1, D_CONV), f32),
        "state_conv_ffn": jax.random.normal(ks[7], (DEPTH, DEC_BATCH, FFN_CONV_WIDTH - 1, D_FF), f32),
        "page_table": page_table,
        "norm1_g": gain(ks[8], (DEPTH, D_MODEL)),
        "w_in": w(ks[9], (DEPTH, D_MODEL, N_IN), D_MODEL),
        "b_f": 2.0 + 0.5 * jax.random.normal(ks[10], (DEPTH, N_HEADS), f32),
        "q_norm_g": gain(ks[11], (DEPTH, HEAD_DIM)),
        "k_norm_g": gain(ks[12], (DEPTH, HEAD_DIM)),
        "conv_a_w": w(ks[13], (DEPTH, CONV_WIDTH, D_CONV), CONV_WIDTH),
        "conv_a_b": small(ks[14], (DEPTH, D_CONV)),
        "ln_a_g": gain(ks[15], (DEPTH, D_CONV)),
        "ln_a_b": small(ks[16], (DEPTH, D_CONV)),
        "w_conv_out": w(ks[17], (DEPTH, D_CONV, D_MODEL), D_CONV),
        "w_attn_out": w(ks[18], (DEPTH, D_ATTN, D_MODEL), D_ATTN),
        "w_o": w(ks[19], (DEPTH, D_MODEL, D_MODEL), D_MODEL),
        "norm2_g": gain(ks[20], (DEPTH, D_MODEL)),
        "w_up": w(ks[21], (DEPTH, D_MODEL, 2 * D_FF), D_MODEL),
        "conv_f_w": w(ks[22], (DEPTH, FFN_CONV_WIDTH, D_FF), FFN_CONV_WIDTH),
        "conv_f_b": small(ks[23], (DEPTH, D_FF)),
        "w_down": w(ks[24], (DEPTH, D_FF, D_MODEL), D_FF),
    }


def reference(x_prompt, x_sample, cache_k, cache_v, cache_logf, state_conv_a, state_conv_ffn, page_table,
              norm1_g, w_in, b_f, q_norm_g, k_norm_g, conv_a_w, conv_a_b, ln_a_g, ln_a_b,
              w_conv_out, w_attn_out, w_o, norm2_g, w_up, conv_f_w, conv_f_b, w_down):
    bp, sp, _ = x_prompt.shape
    bs, ts, _ = x_sample.shape
    n_pages = page_table.shape[1]
    past = n_pages * cache_k.shape[2]
    n_blocks = sp // Q_BLOCK
    pos_p = jnp.arange(sp)
    pos_blocks = pos_p.reshape(n_blocks, Q_BLOCK)
    qpos_s = past + jnp.arange(ts)
    kpos_s = jnp.arange(past + ts)

    xp, xs = x_prompt, x_sample
    kp_l, vp_l, fp_l, cap_l, cfp_l = [], [], [], [], []
    ks_l, vs_l, fs_l, cas_l, cfs_l = [], [], [], [], []

    for l in range(DEPTH):
        xn = _rmsnorm(xp, norm1_g[l])
        a_val, a_gate, q, k, v, logf, g_a, g_b = _in_proj(xn, w_in[l], b_f[l], q_norm_g[l], k_norm_g[l])
        a_out, ca_p = _conformer_branch(a_val, a_gate, jnp.zeros((bp, CONV_WIDTH - 1, D_CONV), xp.dtype),
                                        conv_a_w[l], conv_a_b[l], ln_a_g[l], ln_a_b[l], w_conv_out[l])
        c = jnp.cumsum(logf, axis=1)
        qb = q.reshape(bp, n_blocks, Q_BLOCK, N_HEADS, HEAD_DIM).swapaxes(0, 1)
        cqb = c.reshape(bp, n_blocks, Q_BLOCK, N_HEADS).swapaxes(0, 1)
        ob = lax.map(lambda blk: _fox_attend(blk[0], blk[1], blk[2], k, v, c, pos_p), (qb, cqb, pos_blocks))
        o = ob.swapaxes(0, 1).reshape(bp, sp, D_ATTN)
        xp = xp + _merge(g_a, a_out, g_b, o @ w_attn_out[l], w_o[l])
        f_out, cf_p = _conv_ffn(_rmsnorm(xp, norm2_g[l]), jnp.zeros((bp, FFN_CONV_WIDTH - 1, D_FF), xp.dtype),
                                w_up[l], conv_f_w[l], conv_f_b[l], w_down[l])
        xp = xp + f_out
        kp_l.append(k); vp_l.append(v); fp_l.append(logf); cap_l.append(ca_p); cfp_l.append(cf_p)

        xn = _rmsnorm(xs, norm1_g[l])
        a_val, a_gate, q, k, v, logf, g_a, g_b = _in_proj(xn, w_in[l], b_f[l], q_norm_g[l], k_norm_g[l])
        a_out, ca_s = _conformer_branch(a_val, a_gate, state_conv_a[l],
                                        conv_a_w[l], conv_a_b[l], ln_a_g[l], ln_a_b[l], w_conv_out[l])
        k_past = cache_k[l][page_table].reshape(bs, past, N_HEADS, HEAD_DIM)
        v_past = cache_v[l][page_table].reshape(bs, past, N_HEADS, HEAD_DIM)
        f_past = cache_logf[l][page_table].reshape(bs, past, N_HEADS)
        k_all = jnp.concatenate([k_past.astype(k.dtype), k], axis=1)
        v_all = jnp.concatenate([v_past.astype(v.dtype), v], axis=1)
        c_all = jnp.cumsum(jnp.concatenate([f_past.astype(jnp.float32), logf], axis=1), axis=1)
        o = _fox_attend(q, c_all[:, past:], qpos_s, k_all, v_all, c_all, kpos_s).reshape(bs, ts, D_ATTN)
        xs = xs + _merge(g_a, a_out, g_b, o @ w_attn_out[l], w_o[l])
        f_out, cf_s = _conv_ffn(_rmsnorm(xs, norm2_g[l]), state_conv_ffn[l],
                                w_up[l], conv_f_w[l], conv_f_b[l], w_down[l])
        xs = xs + f_out
        ks_l.append(k); vs_l.append(v); fs_l.append(logf); cas_l.append(ca_s); cfs_l.append(cf_s)

    return (xp, xs,
            jnp.stack(kp_l), jnp.stack(vp_l), jnp.stack(fp_l), jnp.stack(cap_l), jnp.stack(cfp_l),
            jnp.stack(ks_l), jnp.stack(vs_l), jnp.stack(fs_l), jnp.stack(cas_l), jnp.stack(cfs_l))
```

```python
import functools

import jax
import jax.numpy as jnp
from jax import lax
from jax.experimental import pallas as pl
from jax.experimental.pallas import tpu as pltpu

F32 = jnp.float32
BF16 = jnp.bfloat16

EPS = 1e-6
LANES = 128
HEAD_DIM = 64
N_HEADS = 8
D_ATTN = N_HEADS * HEAD_DIM
CONV_HALO = 32
FFN_HALO = 16
ROW_TILE = 512
ATTN_BLOCK = 512
CONV_ROWS = 64
FF_CHUNK = 512
MASKED = -1e30
VMEM_LIMIT_BYTES = 56 * 1024 * 1024


def _compiler_params(*semantics):
    return pltpu.CompilerParams(dimension_semantics=semantics, vmem_limit_bytes=VMEM_LIMIT_BYTES)


def _rms_scale(x, gain):
    return x * lax.rsqrt(jnp.mean(x * x, axis=-1, keepdims=True) + EPS) * gain


def _sigmoid(x):
    return 1.0 / (1.0 + jnp.exp(-x))


def _resident(shape, layer):
    zeros = (0,) * len(shape)
    return pl.BlockSpec((None,) + tuple(shape), lambda *_: (layer,) + zeros)


def _in_proj_body(x_ref, g1_ref, w_ref, bf_ref, qg_ref, kg_ref, seg_ref,
                  u_ref, q_ref, k_ref, kb_ref, v_ref, vb_ref, logf_ref, ga_ref, gb_ref, *, d_conv, d_model):
    xn = _rms_scale(x_ref[...], g1_ref[...]).astype(BF16)

    def proj(lo, width):
        return jnp.dot(xn, w_ref[:, lo:lo + width], preferred_element_type=F32)

    def head_norm(h, gain_ref):
        ms = jnp.dot((h * h).astype(BF16), seg_ref[...], preferred_element_type=F32) * (1.0 / HEAD_DIM)
        return h * lax.rsqrt(ms + EPS) * gain_ref[...]

    off = 0
    a_val = proj(off, d_conv)
    off += d_conv
    u_ref[...] = a_val * _sigmoid(proj(off, d_conv))
    off += d_conv
    q_ref[...] = (head_norm(proj(off, D_ATTN), qg_ref) * HEAD_DIM ** -0.5).astype(BF16)
    off += D_ATTN
    kn = head_norm(proj(off, D_ATTN), kg_ref)
    k_ref[...] = kn
    kb_ref[...] = kn.astype(BF16)
    off += D_ATTN
    v = proj(off, D_ATTN)
    v_ref[...] = v
    vb_ref[...] = v.astype(BF16)
    off += D_ATTN
    ga_ref[...] = _sigmoid(proj(off, d_model)).astype(BF16)
    off += d_model
    gb_ref[...] = _sigmoid(proj(off, d_model)).astype(BF16)
    off += d_model
    z = proj(off, LANES)[:, :N_HEADS] + bf_ref[...]
    logf_ref[...] = jnp.minimum(z, 0.0) - jnp.log1p(jnp.exp(-jnp.abs(z)))


def _in_proj(x, layer, p, tm):
    t, d_model = x.shape
    d_conv = p["d_conv"]
    n_cols = p["w_in"].shape[-1]
    row = lambda width: pl.BlockSpec((tm, width), lambda i: (i, 0))
    out_shape = [
        jax.ShapeDtypeStruct((t, d_conv), F32),
        jax.ShapeDtypeStruct((t, D_ATTN), BF16),
        jax.ShapeDtypeStruct((t, D_ATTN), F32),
        jax.ShapeDtypeStruct((t, D_ATTN), BF16),
        jax.ShapeDtypeStruct((t, D_ATTN), F32),
        jax.ShapeDtypeStruct((t, D_ATTN), BF16),
        jax.ShapeDtypeStruct((t, N_HEADS), F32),
        jax.ShapeDtypeStruct((t, d_model), BF16),
        jax.ShapeDtypeStruct((t, d_model), BF16),
    ]
    return pl.pallas_call(
        functools.partial(_in_proj_body, d_conv=d_conv, d_model=d_model),
        grid=(t // tm,),
        in_specs=[row(d_model), _resident((1, d_model), layer), _resident((d_model, n_cols), layer),
                  _resident((1, N_HEADS), layer), _resident((1, D_ATTN), layer), _resident((1, D_ATTN), layer),
                  pl.BlockSpec((D_ATTN, D_ATTN), lambda i: (0, 0))],
        out_specs=[row(d_conv), row(D_ATTN), row(D_ATTN), row(D_ATTN), row(D_ATTN), row(D_ATTN), row(N_HEADS),
                   row(d_model), row(d_model)],
        out_shape=out_shape,
        compiler_params=_compiler_params("parallel"),
        name="in_proj",
    )(x, p["norm1_g"], p["w_in"], p["b_f"], p["q_norm_g"], p["k_norm_g"], p["seg"])


def _cumsum_body(x_ref, o_ref):
    x = x_ref[0]
    n = x.shape[-1]
    lane = lax.broadcasted_iota(jnp.int32, x.shape, 1)
    shift = 1
    while shift < n:
        x = x + jnp.where(lane >= shift, pltpu.roll(x, shift, axis=1), 0.0)
        shift *= 2
    o_ref[0] = x


def _cumsum_lanes(x):
    b, h, s = x.shape
    spec = pl.BlockSpec((1, h, s), lambda i: (i, 0, 0))
    return pl.pallas_call(_cumsum_body, grid=(b,), in_specs=[spec], out_specs=spec,
                          out_shape=jax.ShapeDtypeStruct(x.shape, F32),
                          compiler_params=_compiler_params("parallel"), name="logf_cumsum")(x)


def _flash_body(q_ref, k_ref, v_ref, cq_ref, ckt_ref, o_ref, *, blk):
    i = pl.program_id(1)
    first_head_lanes = lax.broadcasted_iota(jnp.int32, (1, LANES), 1) < HEAD_DIM
    causal = (lax.broadcasted_iota(jnp.int32, (blk, blk), 1) <= lax.broadcasted_iota(jnp.int32, (blk, blk), 0))
    cq_all = cq_ref[0]
    for pair in range(N_HEADS // 2):
        lanes = slice(pair * LANES, (pair + 1) * LANES)
        qp = q_ref[0, :, lanes]
        zero = jnp.zeros_like(qp)
        q_heads = (jnp.where(first_head_lanes, qp, zero), jnp.where(first_head_lanes, zero, qp))
        cq_heads = tuple(cq_all[:, 2 * pair + h:2 * pair + h + 1] for h in range(2))

        def block(j, carry, masked, lanes=lanes, pair=pair, q_heads=q_heads, cq_heads=cq_heads):
            start = pl.multiple_of(j * blk, blk)
            ks = k_ref[0, pl.ds(start, blk), lanes]
            vs = v_ref[0, pl.ds(start, blk), lanes]
            acc = carry[4]
            stats, scaled = [], []
            for h in range(2):
                m, l = carry[2 * h], carry[2 * h + 1]
                ck = ckt_ref[0, 2 * pair + h:2 * pair + h + 1, pl.ds(start, blk)]
                s = lax.dot_general(q_heads[h], ks, (((1,), (1,)), ((), ())), preferred_element_type=F32)
                s = s + (cq_heads[h] - ck)
                if masked:
                    s = jnp.where(causal, s, MASKED)
                m_new = jnp.maximum(m, jnp.max(s, axis=1, keepdims=True))
                alpha = jnp.exp(m - m_new)
                pr = jnp.exp(s - m_new)
                stats += [m_new, alpha * l + jnp.sum(pr, axis=1, keepdims=True)]
                scaled.append(alpha * acc + jnp.dot(pr.astype(BF16), vs, preferred_element_type=F32))
            return (*stats, jnp.where(first_head_lanes, scaled[0], scaled[1]))

        col = lambda value: jnp.full((blk, 1), value, F32)
        carry = (col(MASKED), col(0.0), col(MASKED), col(0.0), jnp.zeros((blk, LANES), F32))
        carry = lax.fori_loop(0, i, functools.partial(block, masked=False), carry)
        carry = block(i, carry, masked=True)
        o_ref[0, :, lanes] = (carry[4] / jnp.where(first_head_lanes, carry[1], carry[3])).astype(BF16)


def _prompt_attention(q, k, v, c, ct):
    b, s, d = q.shape
    blk = ATTN_BLOCK
    qspec = pl.BlockSpec((1, blk, d), lambda bi, i: (bi, i, 0))
    seq = pl.BlockSpec((1, s, d), lambda bi, i: (bi, 0, 0))
    return pl.pallas_call(
        functools.partial(_flash_body, blk=blk),
        grid=(b, s // blk),
        in_specs=[qspec, seq, seq, pl.BlockSpec((1, blk, N_HEADS), lambda bi, i: (bi, i, 0)),
                  pl.BlockSpec((1, N_HEADS, s), lambda bi, i: (bi, 0, 0))],
        out_specs=qspec,
        out_shape=jax.ShapeDtypeStruct(q.shape, BF16),
        compiler_params=_compiler_params("parallel", "arbitrary"),
        name="prompt_attention",
    )(q, k, v, c, ct)


def _decode_body(pt_ref, q_ref, kn_ref, vn_ref, fn_ref, ck_hbm, cv_hbm, cf_hbm, o_ref,
                 kbuf, vbuf, fbuf, sem, *, layer, n_pages, page):
    b = pl.program_id(0)
    nb = pl.num_programs(0)
    slot = lax.rem(b, 2)
    past = n_pages * page

    def page_copies(seq, slot):
        copies = []
        for pg in range(n_pages):
            phys = pt_ref[seq * n_pages + pg]
            rows = pl.ds(pg * page, page)
            copies.append(pltpu.make_async_copy(ck_hbm.at[layer, phys], kbuf.at[slot, rows], sem.at[0, slot]))
            copies.append(pltpu.make_async_copy(cv_hbm.at[layer, phys], vbuf.at[slot, rows], sem.at[1, slot]))
            copies.append(pltpu.make_async_copy(cf_hbm.at[layer, phys], fbuf.at[slot, :, rows], sem.at[2, slot]))
        return copies

    @pl.when(b == 0)
    def _():
        for cp in page_copies(0, 0):
            cp.start()

    @pl.when(b + 1 < nb)
    def _():
        for cp in page_copies(b + 1, 1 - slot):
            cp.start()

    for cp in page_copies(b, slot):
        cp.wait()

    d = q_ref.shape[-1]
    head_of_lane = lax.shift_right_logical(lax.broadcasted_iota(jnp.int32, (N_HEADS, d), 1), 6)
    own = head_of_lane == lax.broadcasted_iota(jnp.int32, (N_HEADS, d), 0)
    qbd = jnp.where(own, q_ref[0].astype(F32), 0.0)
    s = lax.dot_general(qbd.astype(BF16), kbuf[slot].astype(BF16), (((1,), (1,)), ((), ())),
                        preferred_element_type=F32)

    f = fbuf[slot]
    lane = lax.broadcasted_iota(jnp.int32, f.shape, 1)
    x = jnp.where(lane < past - 1, pltpu.roll(f, past - 1, axis=1), 0.0)
    shift = 1
    while shift < past:
        x = x + jnp.where(lane < past - shift, pltpu.roll(x, past - shift, axis=1), 0.0)
        shift *= 2
    s = s + (x + fn_ref[0])

    s_new = jnp.sum(qbd * kn_ref[0], axis=1, keepdims=True)
    m = jnp.maximum(jnp.max(s, axis=1, keepdims=True), s_new)
    pr = jnp.exp(s - m)
    p_new = jnp.exp(s_new - m)
    denom = jnp.sum(pr, axis=1, keepdims=True) + p_new
    o8 = jnp.dot(pr.astype(BF16), vbuf[slot].astype(BF16), preferred_element_type=F32) + p_new * vn_ref[0]
    o_ref[0] = jnp.sum(jnp.where(own, o8 / denom, 0.0), axis=0, keepdims=True).astype(BF16)


def _decode_attention(q, k_new, v_new, f_new, cache_k, cache_v, cache_ft, page_table, layer):
    bs, d = q.shape
    n_pages = page_table.shape[1]
    page = cache_k.shape[2]
    past = n_pages * page
    per_seq = lambda shape: pl.BlockSpec((1,) + shape, lambda b, pt: (b, 0, 0))
    any_space = pl.BlockSpec(memory_space=pl.ANY)
    grid_spec = pltpu.PrefetchScalarGridSpec(
        num_scalar_prefetch=1,
        grid=(bs,),
        in_specs=[per_seq((1, d)), per_seq((1, d)), per_seq((1, d)), per_seq((N_HEADS, 1)),
                  any_space, any_space, any_space],
        out_specs=per_seq((1, d)),
        scratch_shapes=[pltpu.VMEM((2, past, d), F32), pltpu.VMEM((2, past, d), F32),
                        pltpu.VMEM((2, N_HEADS, past), F32), pltpu.SemaphoreType.DMA((3, 2))],
    )
    out = pl.pallas_call(
        functools.partial(_decode_body, layer=layer, n_pages=n_pages, page=page),
        grid_spec=grid_spec,
        out_shape=jax.ShapeDtypeStruct((bs, 1, d), BF16),
        compiler_params=_compiler_params("arbitrary"),
        name="decode_attention",
    )(page_table.reshape(-1), q.reshape(bs, 1, d), k_new.reshape(bs, 1, d), v_new.reshape(bs, 1, d),
      f_new.reshape(bs, N_HEADS, 1), cache_k, cache_v, cache_ft)
    return out.reshape(bs, d)


def _merge_tail(conv, o_ref, ga_ref, gb_ref, x_ref, lg_ref, lb_ref, wc_ref, wa_ref, wo_ref, out_ref):
    mu = jnp.mean(conv, axis=-1, keepdims=True)
    dev = conv - mu
    y = dev * lax.rsqrt(jnp.mean(dev * dev, axis=-1, keepdims=True) + EPS) * lg_ref[...] + lb_ref[...]
    a_out = jnp.dot((y * _sigmoid(y)).astype(BF16), wc_ref[...], preferred_element_type=F32)
    b_out = jnp.dot(o_ref[...], wa_ref[...], preferred_element_type=F32)
    mix = ga_ref[...].astype(F32) * a_out + gb_ref[...].astype(F32) * b_out
    out_ref[...] = x_ref[...] + jnp.dot(mix.astype(BF16), wo_ref[...], preferred_element_type=F32)


def _merge_prompt_body(u_ref, halo_ref, o_ref, ga_ref, gb_ref, x_ref, cw_ref, cb_ref, lg_ref, lb_ref,
                       wc_ref, wa_ref, wo_ref, out_ref, win_ref, conv_ref, *, tiles_per_seq, width):
    tm = u_ref.shape[0]
    first = lax.rem(pl.program_id(0), tiles_per_seq) == 0
    win_ref[0:CONV_HALO, :] = jnp.where(first, 0.0, halo_ref[...])
    win_ref[CONV_HALO:, :] = u_ref[...]
    lead = CONV_HALO - (width - 1)
    for r0 in range(0, tm, CONV_ROWS):
        acc = jnp.broadcast_to(cb_ref[...], (CONV_ROWS, cb_ref.shape[-1]))
        for j in range(width):
            acc = acc + cw_ref[j:j + 1, :] * win_ref[r0 + lead + j:r0 + lead + j + CONV_ROWS, :]
        conv_ref[r0:r0 + CONV_ROWS, :] = acc
    _merge_tail(conv_ref[...], o_ref, ga_ref, gb_ref, x_ref, lg_ref, lb_ref, wc_ref, wa_ref, wo_ref, out_ref)


def _merge_sample_body(u_ref, st_ref, o_ref, ga_ref, gb_ref, x_ref, cw_ref, cb_ref, lg_ref, lb_ref,
                       wc_ref, wa_ref, wo_ref, out_ref, *, width):
    conv = cb_ref[...] + cw_ref[width - 1:width, :] * u_ref[...]
    for j in range(width - 1):
        conv = conv + cw_ref[j:j + 1, :] * st_ref[j]
    _merge_tail(conv, o_ref, ga_ref, gb_ref, x_ref, lg_ref, lb_ref, wc_ref, wa_ref, wo_ref, out_ref)


def _merge(u, o, ga, gb, x, layer, p, tm, tiles_per_seq=None, state=None):
    t, d_model = x.shape
    d_conv = u.shape[1]
    width = p["conv_a_w"].shape[1]
    row = lambda w: pl.BlockSpec((tm, w), lambda i: (i, 0))
    weights = [_resident((width, d_conv), layer), _resident((1, d_conv), layer), _resident((1, d_conv), layer),
               _resident((1, d_conv), layer), _resident((d_conv, d_model), layer),
               _resident((D_ATTN, d_model), layer), _resident((d_model, d_model), layer)]
    weight_args = (p["conv_a_w"], p["conv_a_b"], p["ln_a_g"], p["ln_a_b"], p["w_conv_out"], p["w_attn_out"], p["w_o"])
    rows = [row(D_ATTN), row(d_model), row(d_model), row(d_model)]
    if state is None:
        halo_blocks = tm // CONV_HALO
        halo = pl.BlockSpec((CONV_HALO, d_conv), lambda i: (jnp.maximum(i * halo_blocks - 1, 0), 0))
        body = functools.partial(_merge_prompt_body, tiles_per_seq=tiles_per_seq, width=width)
        in_specs = [row(d_conv), halo] + rows + weights
        args = (u, u, o, ga, gb, x) + weight_args
        scratch = [pltpu.VMEM((tm + CONV_HALO, d_conv), F32), pltpu.VMEM((tm, d_conv), F32)]
    else:
        body = functools.partial(_merge_sample_body, width=width)
        state_spec = pl.BlockSpec((None, width - 1, tm, d_conv), lambda i: (layer, 0, i, 0))
        in_specs = [row(d_conv), state_spec] + rows + weights
        args = (u, state, o, ga, gb, x) + weight_args
        scratch = []
    return pl.pallas_call(
        body, grid=(t // tm,), in_specs=in_specs, out_specs=row(d_model),
        out_shape=jax.ShapeDtypeStruct((t, d_model), F32), scratch_shapes=scratch,
        compiler_params=_compiler_params("parallel"), name="merge",
    )(*args)


def _ff_chunks(d_ff):
    return [(c0, min(c0 + FF_CHUNK, d_ff)) for c0 in range(0, d_ff, FF_CHUNK)]


def _ffn_prompt_body(x_ref, halo_ref, g2_ref, wup_ref, fw_ref, fb_ref, wdn_ref, out_ref, gtail_ref, gext_ref,
                     *, tiles_per_seq, d_ff):
    tm = x_ref.shape[0]
    first = lax.rem(pl.program_id(0), tiles_per_seq) == 0
    x = x_ref[...]
    xn = _rms_scale(x, g2_ref[...]).astype(BF16)
    xn_halo = _rms_scale(jnp.where(first, 0.0, halo_ref[...]), g2_ref[...]).astype(BF16)
    xn_ext = jnp.concatenate([xn_halo, xn], axis=0)
    acc = x
    for c0, c1 in _ff_chunks(d_ff):
        w = c1 - c0
        gext_ref[:, 0:w] = jnp.dot(xn_ext, wup_ref[:, c0:c1], preferred_element_type=F32)
        up = jnp.dot(xn, wup_ref[:, d_ff + c0:d_ff + c1], preferred_element_type=F32)
        gc = fb_ref[:, c0:c1]
        for j in range(3):
            lo = FFN_HALO - 2 + j
            gc = gc + fw_ref[j:j + 1, c0:c1] * gext_ref[lo:lo + tm, 0:w]
        hidden = (gc * _sigmoid(gc) * up).astype(BF16)
        acc = acc + jnp.dot(hidden, wdn_ref[c0:c1, :], preferred_element_type=F32)
        gtail_ref[:, c0:c1] = gext_ref[FFN_HALO + tm - 8:FFN_HALO + tm, 0:w]
    out_ref[...] = acc


def _ffn_sample_body(x_ref, st_ref, g2_ref, wup_ref, fw_ref, fb_ref, wdn_ref, out_ref, g_ref, *, d_ff):
    x = x_ref[...]
    xn = _rms_scale(x, g2_ref[...]).astype(BF16)
    acc = x
    for c0, c1 in _ff_chunks(d_ff):
        g = jnp.dot(xn, wup_ref[:, c0:c1], preferred_element_type=F32)
        up = jnp.dot(xn, wup_ref[:, d_ff + c0:d_ff + c1], preferred_element_type=F32)
        gc = (fb_ref[:, c0:c1] + fw_ref[0:1, c0:c1] * st_ref[0, :, c0:c1]
              + fw_ref[1:2, c0:c1] * st_ref[1, :, c0:c1] + fw_ref[2:3, c0:c1] * g)
        hidden = (gc * _sigmoid(gc) * up).astype(BF16)
        acc = acc + jnp.dot(hidden, wdn_ref[c0:c1, :], preferred_element_type=F32)
        g_ref[:, c0:c1] = g
    out_ref[...] = acc


def _ffn(x, layer, p, tm, tiles_per_seq=None, state=None):
    t, d_model = x.shape
    d_ff = p["w_down"].shape[1]
    row = pl.BlockSpec((tm, d_model), lambda i: (i, 0))
    weights = [_resident((1, d_model), layer), _resident((d_model, 2 * d_ff), layer), _resident((3, d_ff), layer),
               _resident((1, d_ff), layer), _resident((d_ff, d_model), layer)]
    weight_args = (p["norm2_g"], p["w_up"], p["conv_f_w"], p["conv_f_b"], p["w_down"])
    if state is None:
        halo_blocks = tm // FFN_HALO
        halo = pl.BlockSpec((FFN_HALO, d_model), lambda i: (jnp.maximum(i * halo_blocks - 1, 0), 0))
        body = functools.partial(_ffn_prompt_body, tiles_per_seq=tiles_per_seq, d_ff=d_ff)
        in_specs = [row, halo] + weights
        args = (x, x) + weight_args
        out_specs = [row, pl.BlockSpec((8, d_ff), lambda i: (i, 0))]
        out_shape = [jax.ShapeDtypeStruct((t, d_model), F32), jax.ShapeDtypeStruct((t // tm * 8, d_ff), F32)]
        scratch = [pltpu.VMEM((tm + FFN_HALO, FF_CHUNK), F32)]
    else:
        body = functools.partial(_ffn_sample_body, d_ff=d_ff)
        in_specs = [row, pl.BlockSpec((None, 2, tm, d_ff), lambda i: (layer, 0, i, 0))] + weights
        args = (x, state) + weight_args
        out_specs = [row, pl.BlockSpec((tm, d_ff), lambda i: (i, 0))]
        out_shape = [jax.ShapeDtypeStruct((t, d_model), F32), jax.ShapeDtypeStruct((t, d_ff), F32)]
        scratch = []
    return pl.pallas_call(
        body, grid=(t // tm,), in_specs=in_specs, out_specs=out_specs, out_shape=out_shape, scratch_shapes=scratch,
        compiler_params=_compiler_params("parallel"), name="conv_ffn",
    )(*args)


def kernel(x_prompt, x_sample, cache_k, cache_v, cache_logf, state_conv_a, state_conv_ffn, page_table, norm1_g, w_in, b_f, q_norm_g, k_norm_g, conv_a_w, conv_a_b, ln_a_g, ln_a_b, w_conv_out, w_attn_out, w_o, norm2_g, w_up, conv_f_w, conv_f_b, w_down):
    bp, sp, d_model = x_prompt.shape
    bs, ts, _ = x_sample.shape
    depth, n_phys, page, n_heads, head_dim = cache_k.shape
    d_conv = conv_a_w.shape[-1]
    d_ff = w_down.shape[1]
    assert (n_heads, head_dim, ts) == (N_HEADS, HEAD_DIM, 1)
    assert sp % ROW_TILE == 0 and sp % ATTN_BLOCK == 0 and conv_a_w.shape[1] - 1 <= CONV_HALO

    qkv_end = 2 * d_conv + 3 * D_ATTN
    w_in_cols = jnp.concatenate(
        [w_in[..., :qkv_end], w_in[..., qkv_end + N_HEADS:], w_in[..., qkv_end:qkv_end + N_HEADS],
         jnp.zeros(w_in.shape[:2] + (LANES - N_HEADS,), w_in.dtype)], axis=-1).astype(BF16)
    lane_head = jnp.arange(D_ATTN) // HEAD_DIM
    row3 = lambda a: a[:, None, :]
    p = dict(
        d_conv=d_conv, w_in=w_in_cols, norm1_g=row3(norm1_g), b_f=row3(b_f),
        q_norm_g=row3(jnp.tile(q_norm_g, (1, N_HEADS))), k_norm_g=row3(jnp.tile(k_norm_g, (1, N_HEADS))),
        seg=(lane_head[:, None] == lane_head[None, :]).astype(BF16),
        conv_a_w=conv_a_w, conv_a_b=row3(conv_a_b), ln_a_g=row3(ln_a_g), ln_a_b=row3(ln_a_b),
        w_conv_out=w_conv_out.astype(BF16), w_attn_out=w_attn_out.astype(BF16), w_o=w_o.astype(BF16),
        norm2_g=row3(norm2_g), w_up=w_up.astype(BF16), conv_f_w=conv_f_w, conv_f_b=row3(conv_f_b),
        w_down=w_down.astype(BF16),
    )
    cache_k2 = cache_k.reshape(depth, n_phys, page, D_ATTN)
    cache_v2 = cache_v.reshape(depth, n_phys, page, D_ATTN)
    cache_ft = jnp.swapaxes(cache_logf, 2, 3)
    state_a = jnp.swapaxes(state_conv_a, 1, 2)
    state_f = jnp.swapaxes(state_conv_ffn, 1, 2)

    tiles_per_seq = sp // ROW_TILE
    xp = x_prompt.reshape(bp * sp, d_model)
    xs = x_sample.reshape(bs, d_model)
    outs = [[] for _ in range(10)]
    for layer in range(depth):
        u, q, k, kb, v, vb, logf, ga, gb = _in_proj(xp, layer, p, ROW_TILE)
        logf3 = logf.reshape(bp, sp, N_HEADS)
        ct = _cumsum_lanes(jnp.swapaxes(logf3, 1, 2))
        o = _prompt_attention(q.reshape(bp, sp, D_ATTN), kb.reshape(bp, sp, D_ATTN), vb.reshape(bp, sp, D_ATTN),
                              jnp.swapaxes(ct, 1, 2), ct)
        x1 = _merge(u, o.reshape(bp * sp, D_ATTN), ga, gb, xp, layer, p, ROW_TILE, tiles_per_seq=tiles_per_seq)
        xp, gtail = _ffn(x1, layer, p, ROW_TILE, tiles_per_seq=tiles_per_seq)
        outs[0].append(k.reshape(bp, sp, N_HEADS, HEAD_DIM))
        outs[1].append(v.reshape(bp, sp, N_HEADS, HEAD_DIM))
        outs[2].append(logf3)
        outs[3].append(u.reshape(bp, sp, d_conv)[:, sp - (conv_a_w.shape[1] - 1):])
        outs[4].append(gtail.reshape(bp, tiles_per_seq, 8, d_ff)[:, -1, 6:])

        u, q, k, kb, v, vb, logf, ga, gb = _in_proj(xs, layer, p, bs)
        o = _decode_attention(q, k, v, logf, cache_k2, cache_v2, cache_ft, page_table, layer)
        x1 = _merge(u, o, ga, gb, xs, layer, p, bs, state=state_a)
        xs, g_new = _ffn(x1, layer, p, bs, state=state_f)
        outs[5].append(k.reshape(bs, ts, N_HEADS, HEAD_DIM))
        outs[6].append(v.reshape(bs, ts, N_HEADS, HEAD_DIM))
        outs[7].append(logf.reshape(bs, ts, N_HEADS))
        outs[8].append(jnp.concatenate([state_conv_a[layer][:, 1:], u[:, None, :]], axis=1))
        outs[9].append(jnp.concatenate([state_conv_ffn[layer][:, 1:], g_new[:, None, :]], axis=1))

    return (xp.reshape(bp, sp, d_model), xs.reshape(bs, ts, d_model)) + tuple(jnp.stack(o) for o in outs)
```

```python
import functools

import jax
import jax.numpy as jnp
from jax import lax
from jax.experimental import pallas as pl
from jax.experimental.pallas import tpu as pltpu

F32 = jnp.float32
BF16 = jnp.bfloat16

EPS = 1e-6
LANES = 128
HEAD_DIM = 64
N_HEADS = 8
D_ATTN = N_HEADS * HEAD_DIM
CONV_HALO = 32
FFN_HALO = 16
ROW_TILE = 512
ATTN_BLOCK = 512
CONV_ROWS = 64
FF_CHUNK = 512
MASKED = -1e30
VMEM_LIMIT_BYTES = 56 * 1024 * 1024


def _compiler_params(*semantics):
    return pltpu.CompilerParams(dimension_semantics=semantics, vmem_limit_bytes=VMEM_LIMIT_BYTES)


def _rms_scale(x, gain):
    return x * lax.rsqrt(jnp.mean(x * x, axis=-1, keepdims=True) + EPS) * gain


def _sigmoid(x):
    return 1.0 / (1.0 + jnp.exp(-x))


def _resident(shape, layer):
    zeros = (0,) * len(shape)
    return pl.BlockSpec((None,) + tuple(shape), lambda *_: (layer,) + zeros)


def _in_proj_body(x_ref, g1_ref, w_ref, bf_ref, qg_ref, kg_ref, seg_ref,
                  u_ref, q_ref, k_ref, kb_ref, v_ref, vb_ref, logf_ref, ga_ref, gb_ref, *, d_conv, d_model):
    xn = _rms_scale(x_ref[...], g1_ref[...]).astype(BF16)

    def proj(lo, width):
        return jnp.dot(xn, w_ref[:, lo:lo + width], preferred_element_type=F32)

    def head_norm(h, gain_ref):
        ms = jnp.dot((h * h).astype(BF16), seg_ref[...], preferred_element_type=F32) * (1.0 / HEAD_DIM)
        return h * lax.rsqrt(ms + EPS) * gain_ref[...]

    off = 0
    a_val = proj(off, d_conv)
    off += d_conv
    u_ref[...] = a_val * _sigmoid(proj(off, d_conv))
    off += d_conv
    q_ref[...] = (head_norm(proj(off, D_ATTN), qg_ref) * HEAD_DIM ** -0.5).astype(BF16)
    off += D_ATTN
    kn = head_norm(proj(off, D_ATTN), kg_ref)
    k_ref[...] = kn
    kb_ref[...] = kn.astype(BF16)
    off += D_ATTN
    v = proj(off, D_ATTN)
    v_ref[...] = v
    vb_ref[...] = v.astype(BF16)
    off += D_ATTN
    ga_ref[...] = _sigmoid(proj(off, d_model)).astype(BF16)
    off += d_model
    gb_ref[...] = _sigmoid(proj(off, d_model)).astype(BF16)
    off += d_model
    z = proj(off, LANES)[:, :N_HEADS] + bf_ref[...]
    logf_ref[...] = jnp.minimum(z, 0.0) - jnp.log1p(jnp.exp(-jnp.abs(z)))


def _in_proj(x, layer, p, tm):
    t, d_model = x.shape
    d_conv = p["d_conv"]
    n_cols = p["w_in"].shape[-1]
    row = lambda width: pl.BlockSpec((tm, width), lambda i: (i, 0))
    out_shape = [
        jax.ShapeDtypeStruct((t, d_conv), F32),
        jax.ShapeDtypeStruct((t, D_ATTN), BF16),
        jax.ShapeDtypeStruct((t, D_ATTN), F32),
        jax.ShapeDtypeStruct((t, D_ATTN), BF16),
        jax.ShapeDtypeStruct((t, D_ATTN), F32),
        jax.ShapeDtypeStruct((t, D_ATTN), BF16),
        jax.ShapeDtypeStruct((t, N_HEADS), F32),
        jax.ShapeDtypeStruct((t, d_model), BF16),
        jax.ShapeDtypeStruct((t, d_model), BF16),
    ]
    return pl.pallas_call(
        functools.partial(_in_proj_body, d_conv=d_conv, d_model=d_model),
        grid=(t // tm,),
        in_specs=[row(d_model), _resident((1, d_model), layer), _resident((d_model, n_cols), layer),
                  _resident((1, N_HEADS), layer), _resident((1, D_ATTN), layer), _resident((1, D_ATTN), layer),
                  pl.BlockSpec((D_ATTN, D_ATTN), lambda i: (0, 0))],
        out_specs=[row(d_conv), row(D_ATTN), row(D_ATTN), row(D_ATTN), row(D_ATTN), row(D_ATTN), row(N_HEADS),
                   row(d_model), row(d_model)],
        out_shape=out_shape,
        compiler_params=_compiler_params("parallel"),
        name="in_proj",
    )(x, p["norm1_g"], p["w_in"], p["b_f"], p["q_norm_g"], p["k_norm_g"], p["seg"])


def _cumsum_body(x_ref, o_ref):
    x = x_ref[0]
    n = x.shape[-1]
    lane = lax.broadcasted_iota(jnp.int32, x.shape, 1)
    shift = 1
    while shift < n:
        x = x + jnp.where(lane >= shift, pltpu.roll(x, shift, axis=1), 0.0)
        shift *= 2
    o_ref[0] = x


def _cumsum_lanes(x):
    b, h, s = x.shape
    spec = pl.BlockSpec((1, h, s), lambda i: (i, 0, 0))
    return pl.pallas_call(_cumsum_body, grid=(b,), in_specs=[spec], out_specs=spec,
                          out_shape=jax.ShapeDtypeStruct(x.shape, F32),
                          compiler_params=_compiler_params("parallel"), name="logf_cumsum")(x)


def _flash_body(q_ref, k_ref, v_ref, cq_ref, ckt_ref, o_ref, *, blk):
    i = pl.program_id(1)
    first_head_lanes = lax.broadcasted_iota(jnp.int32, (1, LANES), 1) < HEAD_DIM
    causal = (lax.broadcasted_iota(jnp.int32, (blk, blk), 1) <= lax.broadcasted_iota(jnp.int32, (blk, blk), 0))
    cq_all = cq_ref[0]
    for pair in range(N_HEADS // 2):
        lanes = slice(pair * LANES, (pair + 1) * LANES)
        qp = q_ref[0, :, lanes]
        zero = jnp.zeros_like(qp)
        q_heads = (jnp.where(first_head_lanes, qp, zero), jnp.where(first_head_lanes, zero, qp))
        cq_heads = tuple(cq_all[:, 2 * pair + h:2 * pair + h + 1] for h in range(2))

        def block(j, carry, masked, lanes=lanes, pair=pair, q_heads=q_heads, cq_heads=cq_heads):
            start = pl.multiple_of(j * blk, blk)
            ks = k_ref[0, pl.ds(start, blk), lanes]
            vs = v_ref[0, pl.ds(start, blk), lanes]
            acc = carry[4]
            stats, scaled = [], []
            for h in range(2):
                m, l = carry[2 * h], carry[2 * h + 1]
                ck = ckt_ref[0, 2 * pair + h:2 * pair + h + 1, pl.ds(start, blk)]
                s = lax.dot_general(q_heads[h], ks, (((1,), (1,)), ((), ())), preferred_element_type=F32)
                s = s + (cq_heads[h] - ck)
                if masked:
                    s = jnp.where(causal, s, MASKED)
                m_new = jnp.maximum(m, jnp.max(s, axis=1, keepdims=True))
                alpha = jnp.exp(m - m_new)
                pr = jnp.exp(s - m_new)
                stats += [m_new, alpha * l + jnp.sum(pr, axis=1, keepdims=True)]
                scaled.append(alpha * acc + jnp.dot(pr.astype(BF16), vs, preferred_element_type=F32))
            return (*stats, jnp.where(first_head_lanes, scaled[0], scaled[1]))

        col = lambda value: jnp.full((blk, 1), value, F32)
        carry = (col(MASKED), col(0.0), col(MASKED), col(0.0), jnp.zeros((blk, LANES), F32))
        carry = lax.fori_loop(0, i, functools.partial(block, masked=False), carry)
        carry = block(i, carry, masked=True)
        o_ref[0, :, lanes] = (carry[4] / jnp.where(first_head_lanes, carry[1], carry[3])).astype(BF16)


def _prompt_attention(q, k, v, c, ct):
    b, s, d = q.shape
    blk = ATTN_BLOCK
    qspec = pl.BlockSpec((1, blk, d), lambda bi, i: (bi, i, 0))
    seq = pl.BlockSpec((1, s, d), lambda bi, i: (bi, 0, 0))
    return pl.pallas_call(
        functools.partial(_flash_body, blk=blk),
        grid=(b, s // blk),
        in_specs=[qspec, seq, seq, pl.BlockSpec((1, blk, N_HEADS), lambda bi, i: (bi, i, 0)),
                  pl.BlockSpec((1, N_HEADS, s), lambda bi, i: (bi, 0, 0))],
        out_specs=qspec,
        out_shape=jax.ShapeDtypeStruct(q.shape, BF16),
        compiler_params=_compiler_params("parallel", "arbitrary"),
        name="prompt_attention",
    )(q, k, v, c, ct)


def _decode_body(pt_ref, q_ref, kn_ref, vn_ref, fn_ref, ck_hbm, cv_hbm, cf_hbm, o_ref,
                 kbuf, vbuf, fbuf, sem, *, layer, n_pages, page):
    b = pl.program_id(0)
    nb = pl.num_programs(0)
    slot = lax.rem(b, 2)
    past = n_pages * page

    def page_copies(seq, slot):
        copies = []
        for pg in range(n_pages):
            phys = pt_ref[seq * n_pages + pg]
            cols = pl.ds(pg * page, page)
            copies.append(pltpu.make_async_copy(ck_hbm.at[layer, phys], kbuf.at[slot, :, cols], sem.at[0, slot]))
            copies.append(pltpu.make_async_copy(cv_hbm.at[layer, phys], vbuf.at[slot, :, cols], sem.at[1, slot]))
            copies.append(pltpu.make_async_copy(cf_hbm.at[layer, phys], fbuf.at[slot, :, cols], sem.at[2, slot]))
        return copies

    @pl.when(b == 0)
    def _():
        for cp in page_copies(0, 0):
            cp.start()

    @pl.when(b + 1 < nb)
    def _():
        for cp in page_copies(b + 1, 1 - slot):
            cp.start()

    for cp in page_copies(b, slot):
        cp.wait()

    d = q_ref.shape[-1]
    head_of_lane = lax.shift_right_logical(lax.broadcasted_iota(jnp.int32, (N_HEADS, d), 1), 6)
    own = head_of_lane == lax.broadcasted_iota(jnp.int32, (N_HEADS, d), 0)
    qbd = jnp.where(own, q_ref[0].astype(F32), 0.0)
    s = jnp.dot(qbd.astype(BF16), kbuf[slot].astype(BF16), preferred_element_type=F32)

    f = fbuf[slot]
    lane = lax.broadcasted_iota(jnp.int32, f.shape, 1)
    x = jnp.where(lane < past - 1, pltpu.roll(f, past - 1, axis=1), 0.0)
    shift = 1
    while shift < past:
        x = x + jnp.where(lane < past - shift, pltpu.roll(x, past - shift, axis=1), 0.0)
        shift *= 2
    s = s + (x + fn_ref[0])

    s_new = jnp.sum(qbd * kn_ref[0], axis=1, keepdims=True)
    m = jnp.maximum(jnp.max(s, axis=1, keepdims=True), s_new)
    pr = jnp.exp(s - m)
    p_new = jnp.exp(s_new - m)
    denom = jnp.sum(pr, axis=1, keepdims=True) + p_new
    o8 = lax.dot_general(pr.astype(BF16), vbuf[slot].astype(BF16), (((1,), (1,)), ((), ())),
                         preferred_element_type=F32) + p_new * vn_ref[0]
    o_ref[0] = jnp.sum(jnp.where(own, o8 / denom, 0.0), axis=0, keepdims=True).astype(BF16)


def _decode_attention(q, k_new, v_new, f_new, cache_k, cache_v, cache_ft, page_table, layer):
    bs, d = q.shape
    n_pages = page_table.shape[1]
    page = cache_k.shape[3]
    past = n_pages * page
    per_seq = lambda shape: pl.BlockSpec((1,) + shape, lambda b, pt: (b, 0, 0))
    any_space = pl.BlockSpec(memory_space=pl.ANY)
    grid_spec = pltpu.PrefetchScalarGridSpec(
        num_scalar_prefetch=1,
        grid=(bs,),
        in_specs=[per_seq((1, d)), per_seq((1, d)), per_seq((1, d)), per_seq((N_HEADS, 1)),
                  any_space, any_space, any_space],
        out_specs=per_seq((1, d)),
        scratch_shapes=[pltpu.VMEM((2, d, past), F32), pltpu.VMEM((2, d, past), F32),
                        pltpu.VMEM((2, N_HEADS, past), F32), pltpu.SemaphoreType.DMA((3, 2))],
    )
    out = pl.pallas_call(
        functools.partial(_decode_body, layer=layer, n_pages=n_pages, page=page),
        grid_spec=grid_spec,
        out_shape=jax.ShapeDtypeStruct((bs, 1, d), BF16),
        compiler_params=_compiler_params("arbitrary"),
        name="decode_attention",
    )(page_table.reshape(-1), q.reshape(bs, 1, d), k_new.reshape(bs, 1, d), v_new.reshape(bs, 1, d),
      f_new.reshape(bs, N_HEADS, 1), cache_k, cache_v, cache_ft)
    return out.reshape(bs, d)


def _merge_tail(conv, o_ref, ga_ref, gb_ref, x_ref, lg_ref, lb_ref, wc_ref, wa_ref, wo_ref, out_ref):
    mu = jnp.mean(conv, axis=-1, keepdims=True)
    dev = conv - mu
    y = dev * lax.rsqrt(jnp.mean(dev * dev, axis=-1, keepdims=True) + EPS) * lg_ref[...] + lb_ref[...]
    a_out = jnp.dot((y * _sigmoid(y)).astype(BF16), wc_ref[...], preferred_element_type=F32)
    b_out = jnp.dot(o_ref[...], wa_ref[...], preferred_element_type=F32)
    mix = ga_ref[...].astype(F32) * a_out + gb_ref[...].astype(F32) * b_out
    out_ref[...] = x_ref[...] + jnp.dot(mix.astype(BF16), wo_ref[...], preferred_element_type=F32)


def _merge_prompt_body(u_ref, halo_ref, o_ref, ga_ref, gb_ref, x_ref, cw_ref, cb_ref, lg_ref, lb_ref,
                       wc_ref, wa_ref, wo_ref, out_ref, win_ref, conv_ref, *, tiles_per_seq, width):
    tm = u_ref.shape[0]
    first = lax.rem(pl.program_id(0), tiles_per_seq) == 0
    win_ref[0:CONV_HALO, :] = jnp.where(first, 0.0, halo_ref[...])
    win_ref[CONV_HALO:, :] = u_ref[...]
    lead = CONV_HALO - (width - 1)
    for r0 in range(0, tm, CONV_ROWS):
        acc = jnp.broadcast_to(cb_ref[...], (CONV_ROWS, cb_ref.shape[-1]))
        for j in range(width):
            acc = acc + cw_ref[j:j + 1, :] * win_ref[r0 + lead + j:r0 + lead + j + CONV_ROWS, :]
        conv_ref[r0:r0 + CONV_ROWS, :] = acc
    _merge_tail(conv_ref[...], o_ref, ga_ref, gb_ref, x_ref, lg_ref, lb_ref, wc_ref, wa_ref, wo_ref, out_ref)


def _merge_sample_body(u_ref, st_ref, o_ref, ga_ref, gb_ref, x_ref, cw_ref, cb_ref, lg_ref, lb_ref,
                       wc_ref, wa_ref, wo_ref, out_ref, *, width):
    conv = cb_ref[...] + cw_ref[width - 1:width, :] * u_ref[...]
    for j in range(width - 1):
        conv = conv + cw_ref[j:j + 1, :] * st_ref[j]
    _merge_tail(conv, o_ref, ga_ref, gb_ref, x_ref, lg_ref, lb_ref, wc_ref, wa_ref, wo_ref, out_ref)


def _merge(u, o, ga, gb, x, layer, p, tm, tiles_per_seq=None, state=None):
    t, d_model = x.shape
    d_conv = u.shape[1]
    width = p["conv_a_w"].shape[1]
    row = lambda w: pl.BlockSpec((tm, w), lambda i: (i, 0))
    weights = [_resident((width, d_conv), layer), _resident((1, d_conv), layer), _resident((1, d_conv), layer),
               _resident((1, d_conv), layer), _resident((d_conv, d_model), layer),
               _resident((D_ATTN, d_model), layer), _resident((d_model, d_model), layer)]
    weight_args = (p["conv_a_w"], p["conv_a_b"], p["ln_a_g"], p["ln_a_b"], p["w_conv_out"], p["w_attn_out"], p["w_o"])
    rows = [row(D_ATTN), row(d_model), row(d_model), row(d_model)]
    if state is None:
        halo_blocks = tm // CONV_HALO
        halo = pl.BlockSpec((CONV_HALO, d_conv), lambda i: (jnp.maximum(i * halo_blocks - 1, 0), 0))
        body = functools.partial(_merge_prompt_body, tiles_per_seq=tiles_per_seq, width=width)
        in_specs = [row(d_conv), halo] + rows + weights
        args = (u, u, o, ga, gb, x) + weight_args
        scratch = [pltpu.VMEM((tm + CONV_HALO, d_conv), F32), pltpu.VMEM((tm, d_conv), F32)]
    else:
        body = functools.partial(_merge_sample_body, width=width)
        state_spec = pl.BlockSpec((None, width - 1, tm, d_conv), lambda i: (layer, 0, i, 0))
        in_specs = [row(d_conv), state_spec] + rows + weights
        args = (u, state, o, ga, gb, x) + weight_args
        scratch = []
    return pl.pallas_call(
        body, grid=(t // tm,), in_specs=in_specs, out_specs=row(d_model),
        out_shape=jax.ShapeDtypeStruct((t, d_model), F32), scratch_shapes=scratch,
        compiler_params=_compiler_params("parallel"), name="merge",
    )(*args)


def _ff_chunks(d_ff):
    return [(c0, min(c0 + FF_CHUNK, d_ff)) for c0 in range(0, d_ff, FF_CHUNK)]


def _ffn_prompt_body(x_ref, halo_ref, g2_ref, wup_ref, fw_ref, fb_ref, wdn_ref, out_ref, gtail_ref, gext_ref,
                     *, tiles_per_seq, d_ff):
    tm = x_ref.shape[0]
    first = lax.rem(pl.program_id(0), tiles_per_seq) == 0
    x = x_ref[...]
    xn = _rms_scale(x, g2_ref[...]).astype(BF16)
    xn_halo = _rms_scale(jnp.where(first, 0.0, halo_ref[...]), g2_ref[...]).astype(BF16)
    xn_ext = jnp.concatenate([xn_halo, xn], axis=0)
    acc = x
    for c0, c1 in _ff_chunks(d_ff):
        w = c1 - c0
        gext_ref[:, 0:w] = jnp.dot(xn_ext, wup_ref[:, c0:c1], preferred_element_type=F32)
        up = jnp.dot(xn, wup_ref[:, d_ff + c0:d_ff + c1], preferred_element_type=F32)
        gc = fb_ref[:, c0:c1]
        for j in range(3):
            lo = FFN_HALO - 2 + j
            gc = gc + fw_ref[j:j + 1, c0:c1] * gext_ref[lo:lo + tm, 0:w]
        hidden = (gc * _sigmoid(gc) * up).astype(BF16)
        acc = acc + jnp.dot(hidden, wdn_ref[c0:c1, :], preferred_element_type=F32)
        gtail_ref[:, c0:c1] = gext_ref[FFN_HALO + tm - 8:FFN_HALO + tm, 0:w]
    out_ref[...] = acc


def _ffn_sample_body(x_ref, st_ref, g2_ref, wup_ref, fw_ref, fb_ref, wdn_ref, out_ref, g_ref, *, d_ff):
    x = x_ref[...]
    xn = _rms_scale(x, g2_ref[...]).astype(BF16)
    acc = x
    for c0, c1 in _ff_chunks(d_ff):
        g = jnp.dot(xn, wup_ref[:, c0:c1], preferred_element_type=F32)
        up = jnp.dot(xn, wup_ref[:, d_ff + c0:d_ff + c1], preferred_element_type=F32)
        gc = (fb_ref[:, c0:c1] + fw_ref[0:1, c0:c1] * st_ref[0, :, c0:c1]
              + fw_ref[1:2, c0:c1] * st_ref[1, :, c0:c1] + fw_ref[2:3, c0:c1] * g)
        hidden = (gc * _sigmoid(gc) * up).astype(BF16)
        acc = acc + jnp.dot(hidden, wdn_ref[c0:c1, :], preferred_element_type=F32)
        g_ref[:, c0:c1] = g
    out_ref[...] = acc


def _ffn(x, layer, p, tm, tiles_per_seq=None, state=None):
    t, d_model = x.shape
    d_ff = p["w_down"].shape[1]
    row = pl.BlockSpec((tm, d_model), lambda i: (i, 0))
    weights = [_resident((1, d_model), layer), _resident((d_model, 2 * d_ff), layer), _resident((3, d_ff), layer),
               _resident((1, d_ff), layer), _resident((d_ff, d_model), layer)]
    weight_args = (p["norm2_g"], p["w_up"], p["conv_f_w"], p["conv_f_b"], p["w_down"])
    if state is None:
        halo_blocks = tm // FFN_HALO
        halo = pl.BlockSpec((FFN_HALO, d_model), lambda i: (jnp.maximum(i * halo_blocks - 1, 0), 0))
        body = functools.partial(_ffn_prompt_body, tiles_per_seq=tiles_per_seq, d_ff=d_ff)
        in_specs = [row, halo] + weights
        args = (x, x) + weight_args
        out_specs = [row, pl.BlockSpec((8, d_ff), lambda i: (i, 0))]
        out_shape = [jax.ShapeDtypeStruct((t, d_model), F32), jax.ShapeDtypeStruct((t // tm * 8, d_ff), F32)]
        scratch = [pltpu.VMEM((tm + FFN_HALO, FF_CHUNK), F32)]
    else:
        body = functools.partial(_ffn_sample_body, d_ff=d_ff)
        in_specs = [row, pl.BlockSpec((None, 2, tm, d_ff), lambda i: (layer, 0, i, 0))] + weights
        args = (x, state) + weight_args
        out_specs = [row, pl.BlockSpec((tm, d_ff), lambda i: (i, 0))]
        out_shape = [jax.ShapeDtypeStruct((t, d_model), F32), jax.ShapeDtypeStruct((t, d_ff), F32)]
        scratch = []
    return pl.pallas_call(
        body, grid=(t // tm,), in_specs=in_specs, out_specs=out_specs, out_shape=out_shape, scratch_shapes=scratch,
        compiler_params=_compiler_params("parallel"), name="conv_ffn",
    )(*args)


def kernel(x_prompt, x_sample, cache_k, cache_v, cache_logf, state_conv_a, state_conv_ffn, page_table, norm1_g, w_in, b_f, q_norm_g, k_norm_g, conv_a_w, conv_a_b, ln_a_g, ln_a_b, w_conv_out, w_attn_out, w_o, norm2_g, w_up, conv_f_w, conv_f_b, w_down):
    bp, sp, d_model = x_prompt.shape
    bs, ts, _ = x_sample.shape
    depth, n_phys, page, n_heads, head_dim = cache_k.shape
    d_conv = conv_a_w.shape[-1]
    d_ff = w_down.shape[1]
    assert (n_heads, head_dim, ts) == (N_HEADS, HEAD_DIM, 1)
    assert sp % ROW_TILE == 0 and sp % ATTN_BLOCK == 0 and conv_a_w.shape[1] - 1 <= CONV_HALO

    qkv_end = 2 * d_conv + 3 * D_ATTN
    w_in_cols = jnp.concatenate(
        [w_in[..., :qkv_end], w_in[..., qkv_end + N_HEADS:], w_in[..., qkv_end:qkv_end + N_HEADS],
         jnp.zeros(w_in.shape[:2] + (LANES - N_HEADS,), w_in.dtype)], axis=-1).astype(BF16)
    lane_head = jnp.arange(D_ATTN) // HEAD_DIM
    row3 = lambda a: a[:, None, :]
    p = dict(
        d_conv=d_conv, w_in=w_in_cols, norm1_g=row3(norm1_g), b_f=row3(b_f),
        q_norm_g=row3(jnp.tile(q_norm_g, (1, N_HEADS))), k_norm_g=row3(jnp.tile(k_norm_g, (1, N_HEADS))),
        seg=(lane_head[:, None] == lane_head[None, :]).astype(BF16),
        conv_a_w=conv_a_w, conv_a_b=row3(conv_a_b), ln_a_g=row3(ln_a_g), ln_a_b=row3(ln_a_b),
        w_conv_out=w_conv_out.astype(BF16), w_attn_out=w_attn_out.astype(BF16), w_o=w_o.astype(BF16),
        norm2_g=row3(norm2_g), w_up=w_up.astype(BF16), conv_f_w=conv_f_w, conv_f_b=row3(conv_f_b),
        w_down=w_down.astype(BF16),
    )
    cache_k2 = jnp.transpose(cache_k, (0, 1, 3, 4, 2)).reshape(depth, n_phys, D_ATTN, page)
    cache_v2 = jnp.transpose(cache_v, (0, 1, 3, 4, 2)).reshape(depth, n_phys, D_ATTN, page)
    cache_ft = jnp.swapaxes(cache_logf, 2, 3)
    state_a = jnp.swapaxes(state_conv_a, 1, 2)
    state_f = jnp.swapaxes(state_conv_ffn, 1, 2)

    tiles_per_seq = sp // ROW_TILE
    xp = x_prompt.reshape(bp * sp, d_model)
    xs = x_sample.reshape(bs, d_model)
    outs = [[] for _ in range(10)]
    for layer in range(depth):
        u, q, k, kb, v, vb, logf, ga, gb = _in_proj(xp, layer, p, ROW_TILE)
        logf3 = logf.reshape(bp, sp, N_HEADS)
        ct = _cumsum_lanes(jnp.swapaxes(logf3, 1, 2))
        o = _prompt_attention(q.reshape(bp, sp, D_ATTN), kb.reshape(bp, sp, D_ATTN), vb.reshape(bp, sp, D_ATTN),
                              jnp.swapaxes(ct, 1, 2), ct)
        x1 = _merge(u, o.reshape(bp * sp, D_ATTN), ga, gb, xp, layer, p, ROW_TILE, tiles_per_seq=tiles_per_seq)
        xp, gtail = _ffn(x1, layer, p, ROW_TILE, tiles_per_seq=tiles_per_seq)
        outs[0].append(k.reshape(bp, sp, N_HEADS, HEAD_DIM))
        outs[1].append(v.reshape(bp, sp, N_HEADS, HEAD_DIM))
        outs[2].append(logf3)
        outs[3].append(u.reshape(bp, sp, d_conv)[:, sp - (conv_a_w.shape[1] - 1):])
        outs[4].append(gtail.reshape(bp, tiles_per_seq, 8, d_ff)[:, -1, 6:])

        u, q, k, kb, v, vb, logf, ga, gb = _in_proj(xs, layer, p, bs)
        o = _decode_attention(q, k, v, logf, cache_k2, cache_v2, cache_ft, page_table, layer)
        x1 = _merge(u, o, ga, gb, xs, layer, p, bs, state=state_a)
        xs, g_new = _ffn(x1, layer, p, bs, state=state_f)
        outs[5].append(k.reshape(bs, ts, N_HEADS, HEAD_DIM))
        outs[6].append(v.reshape(bs, ts, N_HEADS, HEAD_DIM))
        outs[7].append(logf.reshape(bs, ts, N_HEADS))
        outs[8].append(jnp.concatenate([state_conv_a[layer][:, 1:], u[:, None, :]], axis=1))
        outs[9].append(jnp.concatenate([state_conv_ffn[layer][:, 1:], g_new[:, None, :]], axis=1))

    return (xp.reshape(bp, sp, d_model), xs.reshape(bs, ts, d_model)) + tuple(jnp.stack(o) for o in outs)
```

```python
import functools
import math

import numpy as np
import jax
import jax.numpy as jnp
from jax import lax
from jax.experimental import pallas as pl
from jax.experimental.pallas import tpu as pltpu

F32 = jnp.float32
BF16 = jnp.bfloat16

EPS = 1e-6
LOG2E = math.log2(math.e)
LANES = 128
SUBLANES = 8
HEAD_DIM = 64
N_HEADS = 8
D_ATTN = N_HEADS * HEAD_DIM
D_PAD = N_HEADS * LANES
CONV_HALO = 32
FFN_HALO = 16
ROW_TILE = 512
ATTN_BLOCK = 512
HEADS_PER_PASS = 8
ONES_ROWS = 16
CONV_ROWS = 64
FF_CHUNK = 512
MASKED = -1e30
VMEM_LIMIT_BYTES = 56 * 1024 * 1024


def _compiler_params(*semantics):
    return pltpu.CompilerParams(dimension_semantics=semantics, vmem_limit_bytes=VMEM_LIMIT_BYTES)


def _rms_scale(x, gain):
    return x * lax.rsqrt(jnp.mean(x * x, axis=-1, keepdims=True) + EPS) * gain


def _sigmoid(x):
    return 1.0 / (1.0 + jnp.exp(-x))


def _resident(shape, layer):
    zeros = (0,) * len(shape)
    return pl.BlockSpec((None,) + tuple(shape), lambda *_: (layer,) + zeros, pipeline_mode=pl.Buffered(1))


def _constant(shape):
    zeros = (0,) * len(shape)
    return pl.BlockSpec(tuple(shape), lambda *_: zeros, pipeline_mode=pl.Buffered(1))


def _split3(x):
    hi = x.astype(BF16)
    rest = x - hi.astype(F32)
    mid = rest.astype(BF16)
    return hi, mid, (rest - mid.astype(F32)).astype(BF16)


def _pad_heads(x, cols):
    low = lax.broadcasted_iota(jnp.int32, (1, LANES), 1) < HEAD_DIM
    tiles = []
    for pair in range(N_HEADS // 2):
        tile = x[:, pair * LANES:(pair + 1) * LANES]
        for h, src in ((2 * pair, tile), (2 * pair + 1, pltpu.roll(tile, HEAD_DIM, axis=1))):
            tiles.append(jnp.where(low, src, cols[:, h * LANES:(h + 1) * LANES]))
    return jnp.concatenate(tiles, axis=1)


def _in_proj_body(x_ref, g1_ref, w_ref, bf_ref, qg_ref, kg_ref, seg_ref, *rest, d_conv, d_model, tiles_per_seq):
    prompt = tiles_per_seq is not None
    if prompt:
        tri_ref, pq_ref, qc_ref, pk_ref, kc_ref = rest[:5]
        rest = rest[5:]
    u_ref, q_ref, k_ref, kb_ref, v_ref, vb_ref, logf_ref, ga_ref, gb_ref = rest[:9]
    tm = x_ref.shape[0]
    xn = _rms_scale(x_ref[...], g1_ref[...]).astype(BF16)

    def proj(lo, width):
        return jnp.dot(xn, w_ref[:, lo:lo + width], preferred_element_type=F32)

    def head_norm(h, gain_ref):
        ms = jnp.dot((h * h).astype(BF16), seg_ref[...], preferred_element_type=F32) * (1.0 / HEAD_DIM)
        return h * lax.rsqrt(ms + EPS) * gain_ref[...]

    off = 0
    a_val = proj(off, d_conv)
    off += d_conv
    u_ref[...] = a_val * _sigmoid(proj(off, d_conv))
    off += d_conv
    qn = head_norm(proj(off, D_ATTN), qg_ref)
    off += D_ATTN
    kn = head_norm(proj(off, D_ATTN), kg_ref)
    k_ref[...] = kn
    off += D_ATTN
    v = proj(off, D_ATTN)
    v_ref[...] = v
    vb_ref[...] = v.astype(BF16)
    off += D_ATTN
    ga_ref[...] = _sigmoid(proj(off, d_model)).astype(BF16)
    off += d_model
    gb_ref[...] = _sigmoid(proj(off, d_model)).astype(BF16)
    off += d_model
    z = proj(off, LANES) + bf_ref[...]
    logf = jnp.minimum(z, 0.0) - jnp.log1p(jnp.exp(-jnp.abs(z)))
    logf_ref[...] = logf[:, :N_HEADS]

    if not prompt:
        q_ref[...] = (qn * HEAD_DIM ** -0.5).astype(BF16)
        kb_ref[...] = kn.astype(BF16)
        return

    carry_ref = rest[9]

    @pl.when(lax.rem(pl.program_id(0), tiles_per_seq) == 0)
    def _():
        carry_ref[...] = jnp.zeros_like(carry_ref)

    c = carry_ref[...]
    for piece in _split3(logf * LOG2E):
        c = c + jnp.dot(tri_ref[...], piece, preferred_element_type=F32)
    carry_ref[...] = c[tm - 1:tm, :]
    pieces = jnp.concatenate(_split3(c), axis=1)
    q_cols = jnp.dot(pieces, pq_ref[...], preferred_element_type=F32) + qc_ref[...]
    k_cols = jnp.dot(pieces, pk_ref[...], preferred_element_type=F32) + kc_ref[...]
    q_ref[...] = _pad_heads(qn * (LOG2E * HEAD_DIM ** -0.5), q_cols).astype(BF16)
    kb_ref[...] = _pad_heads(kn, k_cols).astype(BF16)


def _bias_placement():
    pq = np.zeros((3 * LANES, D_PAD), np.float32)
    pk = np.zeros((3 * LANES, D_PAD), np.float32)
    qc = np.zeros((1, D_PAD), np.float32)
    kc = np.zeros((1, D_PAD), np.float32)
    for h in range(N_HEADS):
        base = h * LANES + HEAD_DIM
        for j in range(3):
            pq[j * LANES + h, base + j] = 1.0
            kc[0, base + j] = 1.0
            pk[j * LANES + h, base + 3 + j] = -1.0
            qc[0, base + 3 + j] = 1.0
    return jnp.asarray(pq, BF16), jnp.asarray(qc), jnp.asarray(pk, BF16), jnp.asarray(kc)


def _in_proj(x, layer, p, tm, tiles_per_seq=None):
    t, d_model = x.shape
    d_conv = p["d_conv"]
    n_cols = p["w_in"].shape[-1]
    prompt = tiles_per_seq is not None
    qk_width = D_PAD if prompt else D_ATTN
    row = lambda width: pl.BlockSpec((tm, width), lambda i: (i, 0))
    out_shape = [
        jax.ShapeDtypeStruct((t, d_conv), F32),
        jax.ShapeDtypeStruct((t, qk_width), BF16),
        jax.ShapeDtypeStruct((t, D_ATTN), F32),
        jax.ShapeDtypeStruct((t, qk_width), BF16),
        jax.ShapeDtypeStruct((t, D_ATTN), F32),
        jax.ShapeDtypeStruct((t, D_ATTN), BF16),
        jax.ShapeDtypeStruct((t, N_HEADS), F32),
        jax.ShapeDtypeStruct((t, d_model), BF16),
        jax.ShapeDtypeStruct((t, d_model), BF16),
    ]
    in_specs = [row(d_model), _resident((1, d_model), layer), _resident((d_model, n_cols), layer),
                _resident((1, LANES), layer), _resident((1, D_ATTN), layer), _resident((1, D_ATTN), layer),
                _constant((D_ATTN, D_ATTN))]
    args = [x, p["norm1_g"], p["w_in"], p["b_f"], p["q_norm_g"], p["k_norm_g"], p["seg"]]
    scratch = []
    if prompt:
        tri = jnp.asarray(np.tril(np.ones((tm, tm), np.float32)), BF16)
        consts = (tri,) + _bias_placement()
        in_specs += [_constant(c.shape) for c in consts]
        args += list(consts)
        scratch = [pltpu.VMEM((1, LANES), F32)]
    return pl.pallas_call(
        functools.partial(_in_proj_body, d_conv=d_conv, d_model=d_model, tiles_per_seq=tiles_per_seq),
        grid=(t // tm,),
        in_specs=in_specs,
        out_specs=[row(d_conv), row(qk_width), row(D_ATTN), row(qk_width), row(D_ATTN), row(D_ATTN), row(N_HEADS),
                   row(d_model), row(d_model)],
        out_shape=out_shape,
        scratch_shapes=scratch,
        compiler_params=_compiler_params("arbitrary"),
        name="in_proj",
    )(*args)


def _flash_body(qt_ref, k_ref, vt_ref, ot_ref, *, blk):
    i = pl.program_id(1)
    causal = (lax.broadcasted_iota(jnp.int32, (blk, blk), 0) <= lax.broadcasted_iota(jnp.int32, (blk, blk), 1))
    ones_rows = jnp.ones((ONES_ROWS, blk), BF16)
    for group in range(N_HEADS // HEADS_PER_PASS):
        heads = tuple(range(group * HEADS_PER_PASS, (group + 1) * HEADS_PER_PASS))
        q_heads = [qt_ref[0, h * LANES:(h + 1) * LANES, :] for h in heads]

        def block(j, carry, masked, heads=heads, q_heads=q_heads):
            start = pl.multiple_of(j * blk, blk)
            out = []
            for idx, h in enumerate(heads):
                m, acc = carry[2 * idx:2 * idx + 2]
                st = jnp.dot(k_ref[0, pl.ds(start, blk), h * LANES:(h + 1) * LANES], q_heads[idx],
                             preferred_element_type=F32)
                if masked:
                    st = jnp.where(causal, st, MASKED)
                m_new = jnp.maximum(m, jnp.max(st, axis=0, keepdims=True))
                pr = jnp.exp2(st - m_new).astype(BF16)
                vt = vt_ref[0, h * HEAD_DIM:(h + 1) * HEAD_DIM, pl.ds(start, blk)]
                pv = jnp.dot(jnp.concatenate([vt, ones_rows], axis=0), pr, preferred_element_type=F32)
                out += [m_new, jnp.exp2(m - m_new) * acc + pv]
            return tuple(out)

        init = (jnp.full((1, blk), MASKED, F32), jnp.zeros((HEAD_DIM + ONES_ROWS, blk), F32)) * len(heads)
        carry = lax.fori_loop(0, i, functools.partial(block, masked=False), init)
        carry = block(i, carry, masked=True)
        for idx, h in enumerate(heads):
            acc = carry[2 * idx + 1]
            ot_ref[0, h * HEAD_DIM:(h + 1) * HEAD_DIM, :] = (acc[:HEAD_DIM] / acc[HEAD_DIM:HEAD_DIM + 1]).astype(BF16)


def _prompt_attention(q_aug, k_aug, v):
    b, s, d = v.shape
    blk = ATTN_BLOCK
    qt = jnp.swapaxes(q_aug, 1, 2)
    vt = jnp.swapaxes(v, 1, 2)
    ot = pl.pallas_call(
        functools.partial(_flash_body, blk=blk),
        grid=(b, s // blk),
        in_specs=[pl.BlockSpec((1, D_PAD, blk), lambda bi, i: (bi, 0, i)),
                  pl.BlockSpec((1, s, D_PAD), lambda bi, i: (bi, 0, 0)),
                  pl.BlockSpec((1, d, s), lambda bi, i: (bi, 0, 0))],
        out_specs=pl.BlockSpec((1, d, blk), lambda bi, i: (bi, 0, i)),
        out_shape=jax.ShapeDtypeStruct((b, d, s), BF16),
        compiler_params=_compiler_params("parallel", "arbitrary"),
        name="prompt_attention",
    )(qt, k_aug, vt)
    return jnp.swapaxes(ot, 1, 2)


def _decode_body(pt_ref, q_ref, kn_ref, vn_ref, fn_ref, ck_hbm, cv_hbm, cf_hbm, o_ref,
                 kbuf, vbuf, fbuf, sem, *, layer, n_pages, page):
    b = pl.program_id(0)
    nb = pl.num_programs(0)
    slot = lax.rem(b, 2)
    past = n_pages * page

    def page_copies(seq, slot):
        copies = []
        for pg in range(n_pages):
            phys = pt_ref[seq * n_pages + pg]
            cols = pl.ds(pg * page, page)
            copies.append(pltpu.make_async_copy(ck_hbm.at[layer, phys], kbuf.at[slot, :, cols], sem.at[0, slot]))
            copies.append(pltpu.make_async_copy(cv_hbm.at[layer, phys], vbuf.at[slot, :, cols], sem.at[1, slot]))
            copies.append(pltpu.make_async_copy(cf_hbm.at[layer, phys], fbuf.at[slot, :, cols], sem.at[2, slot]))
        return copies

    @pl.when(b == 0)
    def _():
        for cp in page_copies(0, 0):
            cp.start()

    @pl.when(b + 1 < nb)
    def _():
        for cp in page_copies(b + 1, 1 - slot):
            cp.start()

    for cp in page_copies(b, slot):
        cp.wait()

    d = q_ref.shape[-1]
    head_of_lane = lax.shift_right_logical(lax.broadcasted_iota(jnp.int32, (N_HEADS, d), 1), 6)
    own = head_of_lane == lax.broadcasted_iota(jnp.int32, (N_HEADS, d), 0)
    qbd = jnp.where(own, q_ref[0].astype(F32), 0.0)
    s = jnp.dot(qbd.astype(BF16), kbuf[slot].astype(BF16), preferred_element_type=F32)

    f = fbuf[slot]
    lane = lax.broadcasted_iota(jnp.int32, f.shape, 1)
    x = jnp.where(lane < past - 1, pltpu.roll(f, past - 1, axis=1), 0.0)
    shift = 1
    while shift < past:
        x = x + jnp.where(lane < past - shift, pltpu.roll(x, past - shift, axis=1), 0.0)
        shift *= 2
    s = s + (x + fn_ref[0])

    s_new = jnp.sum(qbd * kn_ref[0], axis=1, keepdims=True)
    m = jnp.maximum(jnp.max(s, axis=1, keepdims=True), s_new)
    pr = jnp.exp(s - m)
    p_new = jnp.exp(s_new - m)
    denom = jnp.sum(pr, axis=1, keepdims=True) + p_new
    o8 = lax.dot_general(pr.astype(BF16), vbuf[slot].astype(BF16), (((1,), (1,)), ((), ())),
                         preferred_element_type=F32) + p_new * vn_ref[0]
    o_ref[0] = jnp.sum(jnp.where(own, o8 / denom, 0.0), axis=0, keepdims=True).astype(BF16)


def _decode_attention(q, k_new, v_new, f_new, cache_k, cache_v, cache_ft, page_table, layer):
    bs, d = q.shape
    n_pages = page_table.shape[1]
    page = cache_k.shape[3]
    past = n_pages * page
    per_seq = lambda shape: pl.BlockSpec((1,) + shape, lambda b, pt: (b, 0, 0))
    any_space = pl.BlockSpec(memory_space=pl.ANY)
    grid_spec = pltpu.PrefetchScalarGridSpec(
        num_scalar_prefetch=1,
        grid=(bs,),
        in_specs=[per_seq((1, d)), per_seq((1, d)), per_seq((1, d)), per_seq((N_HEADS, 1)),
                  any_space, any_space, any_space],
        out_specs=per_seq((1, d)),
        scratch_shapes=[pltpu.VMEM((2, d, past), F32), pltpu.VMEM((2, d, past), F32),
                        pltpu.VMEM((2, N_HEADS, past), F32), pltpu.SemaphoreType.DMA((3, 2))],
    )
    out = pl.pallas_call(
        functools.partial(_decode_body, layer=layer, n_pages=n_pages, page=page),
        grid_spec=grid_spec,
        out_shape=jax.ShapeDtypeStruct((bs, 1, d), BF16),
        compiler_params=_compiler_params("arbitrary"),
        name="decode_attention",
    )(page_table.reshape(-1), q.reshape(bs, 1, d), k_new.reshape(bs, 1, d), v_new.reshape(bs, 1, d),
      f_new.reshape(bs, N_HEADS, 1), cache_k, cache_v, cache_ft)
    return out.reshape(bs, d)


def _merge_tail(conv, o_ref, ga_ref, gb_ref, x_ref, lg_ref, lb_ref, wc_ref, wa_ref, wo_ref, out_ref):
    mu = jnp.mean(conv, axis=-1, keepdims=True)
    dev = conv - mu
    y = dev * lax.rsqrt(jnp.mean(dev * dev, axis=-1, keepdims=True) + EPS) * lg_ref[...] + lb_ref[...]
    a_out = jnp.dot((y * _sigmoid(y)).astype(BF16), wc_ref[...], preferred_element_type=F32)
    b_out = jnp.dot(o_ref[...], wa_ref[...], preferred_element_type=F32)
    mix = ga_ref[...].astype(F32) * a_out + gb_ref[...].astype(F32) * b_out
    out_ref[...] = x_ref[...] + jnp.dot(mix.astype(BF16), wo_ref[...], preferred_element_type=F32)


def _merge_prompt_body(u_ref, halo_ref, o_ref, ga_ref, gb_ref, x_ref, cw_ref, cb_ref, lg_ref, lb_ref,
                       wc_ref, wa_ref, wo_ref, out_ref, win_ref, conv_ref, *, tiles_per_seq, width):
    tm = u_ref.shape[0]
    first = lax.rem(pl.program_id(0), tiles_per_seq) == 0
    win_ref[0, 0:CONV_HALO, :] = jnp.where(first, 0.0, halo_ref[...])
    win_ref[0, CONV_HALO:, :] = u_ref[...]
    span = tm + CONV_HALO - SUBLANES
    for r in range(1, SUBLANES):
        win_ref[r, 0:span, :] = win_ref[0, r:r + span, :]
    lead = CONV_HALO - (width - 1)
    for r0 in range(0, tm, CONV_ROWS):
        acc = jnp.broadcast_to(cb_ref[...], (CONV_ROWS, cb_ref.shape[-1]))
        for j in range(width):
            r = (lead + j) % SUBLANES
            base = r0 + lead + j - r
            acc = acc + cw_ref[j:j + 1, :] * win_ref[r, base:base + CONV_ROWS, :]
        conv_ref[r0:r0 + CONV_ROWS, :] = acc
    _merge_tail(conv_ref[...], o_ref, ga_ref, gb_ref, x_ref, lg_ref, lb_ref, wc_ref, wa_ref, wo_ref, out_ref)


def _merge_sample_body(u_ref, st_ref, o_ref, ga_ref, gb_ref, x_ref, cw_ref, cb_ref, lg_ref, lb_ref,
                       wc_ref, wa_ref, wo_ref, out_ref, *, width):
    conv = cb_ref[...] + cw_ref[width - 1:width, :] * u_ref[...]
    for j in range(width - 1):
        conv = conv + cw_ref[j:j + 1, :] * st_ref[j]
    _merge_tail(conv, o_ref, ga_ref, gb_ref, x_ref, lg_ref, lb_ref, wc_ref, wa_ref, wo_ref, out_ref)


def _merge(u, o, ga, gb, x, layer, p, tm, tiles_per_seq=None, state=None):
    t, d_model = x.shape
    d_conv = u.shape[1]
    width = p["conv_a_w"].shape[1]
    row = lambda w: pl.BlockSpec((tm, w), lambda i: (i, 0))
    weights = [_resident((width, d_conv), layer), _resident((1, d_conv), layer), _resident((1, d_conv), layer),
               _resident((1, d_conv), layer), _resident((d_conv, d_model), layer),
               _resident((D_ATTN, d_model), layer), _resident((d_model, d_model), layer)]
    weight_args = (p["conv_a_w"], p["conv_a_b"], p["ln_a_g"], p["ln_a_b"], p["w_conv_out"], p["w_attn_out"], p["w_o"])
    rows = [row(D_ATTN), row(d_model), row(d_model), row(d_model)]
    if state is None:
        halo_blocks = tm // CONV_HALO
        halo = pl.BlockSpec((CONV_HALO, d_conv), lambda i: (jnp.maximum(i * halo_blocks - 1, 0), 0))
        body = functools.partial(_merge_prompt_body, tiles_per_seq=tiles_per_seq, width=width)
        in_specs = [row(d_conv), halo] + rows + weights
        args = (u, u, o, ga, gb, x) + weight_args
        scratch = [pltpu.VMEM((SUBLANES, tm + CONV_HALO, d_conv), F32), pltpu.VMEM((tm, d_conv), F32)]
    else:
        body = functools.partial(_merge_sample_body, width=width)
        state_spec = pl.BlockSpec((None, width - 1, tm, d_conv), lambda i: (layer, 0, i, 0))
        in_specs = [row(d_conv), state_spec] + rows + weights
        args = (u, state, o, ga, gb, x) + weight_args
        scratch = []
    return pl.pallas_call(
        body, grid=(t // tm,), in_specs=in_specs, out_specs=row(d_model),
        out_shape=jax.ShapeDtypeStruct((t, d_model), F32), scratch_shapes=scratch,
        compiler_params=_compiler_params("parallel"), name="merge",
    )(*args)


def _ff_chunks(d_ff):
    return [(c0, min(c0 + FF_CHUNK, d_ff)) for c0 in range(0, d_ff, FF_CHUNK)]


def _ffn_prompt_body(x_ref, halo_ref, g2_ref, wup_ref, fw_ref, fb_ref, wdn_ref, out_ref, gtail_ref, gext_ref,
                     hidden_ref, *, tiles_per_seq, d_ff):
    tm = x_ref.shape[0]
    first = lax.rem(pl.program_id(0), tiles_per_seq) == 0
    x = x_ref[...]
    xn = _rms_scale(x, g2_ref[...]).astype(BF16)
    xn_halo = _rms_scale(jnp.where(first, 0.0, halo_ref[...]), g2_ref[...]).astype(BF16)
    xn_ext = jnp.concatenate([xn_halo, xn], axis=0)
    for c0, c1 in _ff_chunks(d_ff):
        gext_ref[:, c0:c1] = jnp.dot(xn_ext, wup_ref[:, c0:c1], preferred_element_type=F32)
        up = jnp.dot(xn, wup_ref[:, d_ff + c0:d_ff + c1], preferred_element_type=F32)
        gc = fb_ref[:, c0:c1]
        for j in range(3):
            lo = FFN_HALO - 2 + j
            gc = gc + fw_ref[j:j + 1, c0:c1] * gext_ref[lo:lo + tm, c0:c1]
        hidden_ref[:, c0:c1] = (gc * _sigmoid(gc) * up).astype(BF16)
        gtail_ref[:, c0:c1] = gext_ref[FFN_HALO + tm - 8:FFN_HALO + tm, c0:c1]
    out_ref[...] = x + jnp.dot(hidden_ref[...], wdn_ref[...], preferred_element_type=F32)


def _ffn_sample_body(x_ref, st_ref, g2_ref, wup_ref, fw_ref, fb_ref, wdn_ref, out_ref, g_ref, *, d_ff):
    x = x_ref[...]
    xn = _rms_scale(x, g2_ref[...]).astype(BF16)
    acc = x
    for c0, c1 in _ff_chunks(d_ff):
        g = jnp.dot(xn, wup_ref[:, c0:c1], preferred_element_type=F32)
        up = jnp.dot(xn, wup_ref[:, d_ff + c0:d_ff + c1], preferred_element_type=F32)
        gc = (fb_ref[:, c0:c1] + fw_ref[0:1, c0:c1] * st_ref[0, :, c0:c1]
              + fw_ref[1:2, c0:c1] * st_ref[1, :, c0:c1] + fw_ref[2:3, c0:c1] * g)
        hidden = (gc * _sigmoid(gc) * up).astype(BF16)
        acc = acc + jnp.dot(hidden, wdn_ref[c0:c1, :], preferred_element_type=F32)
        g_ref[:, c0:c1] = g
    out_ref[...] = acc


def _ffn(x, layer, p, tm, tiles_per_seq=None, state=None):
    t, d_model = x.shape
    d_ff = p["w_down"].shape[1]
    row = pl.BlockSpec((tm, d_model), lambda i: (i, 0))
    weights = [_resident((1, d_model), layer), _resident((d_model, 2 * d_ff), layer), _resident((3, d_ff), layer),
               _resident((1, d_ff), layer), _resident((d_ff, d_model), layer)]
    weight_args = (p["norm2_g"], p["w_up"], p["conv_f_w"], p["conv_f_b"], p["w_down"])
    if state is None:
        halo_blocks = tm // FFN_HALO
        halo = pl.BlockSpec((FFN_HALO, d_model), lambda i: (jnp.maximum(i * halo_blocks - 1, 0), 0))
        body = functools.partial(_ffn_prompt_body, tiles_per_seq=tiles_per_seq, d_ff=d_ff)
        in_specs = [row, halo] + weights
        args = (x, x) + weight_args
        out_specs = [row, pl.BlockSpec((8, d_ff), lambda i: (i, 0))]
        out_shape = [jax.ShapeDtypeStruct((t, d_model), F32), jax.ShapeDtypeStruct((t // tm * 8, d_ff), F32)]
        scratch = [pltpu.VMEM((tm + FFN_HALO, d_ff), F32), pltpu.VMEM((tm, d_ff), BF16)]
    else:
        body = functools.partial(_ffn_sample_body, d_ff=d_ff)
        in_specs = [row, pl.BlockSpec((None, 2, tm, d_ff), lambda i: (layer, 0, i, 0))] + weights
        args = (x, state) + weight_args
        out_specs = [row, pl.BlockSpec((tm, d_ff), lambda i: (i, 0))]
        out_shape = [jax.ShapeDtypeStruct((t, d_model), F32), jax.ShapeDtypeStruct((t, d_ff), F32)]
        scratch = []
    return pl.pallas_call(
        body, grid=(t // tm,), in_specs=in_specs, out_specs=out_specs, out_shape=out_shape, scratch_shapes=scratch,
        compiler_params=_compiler_params("parallel"), name="conv_ffn",
    )(*args)


def kernel(x_prompt, x_sample, cache_k, cache_v, cache_logf, state_conv_a, state_conv_ffn, page_table, norm1_g, w_in, b_f, q_norm_g, k_norm_g, conv_a_w, conv_a_b, ln_a_g, ln_a_b, w_conv_out, w_attn_out, w_o, norm2_g, w_up, conv_f_w, conv_f_b, w_down):
    bp, sp, d_model = x_prompt.shape
    bs, ts, _ = x_sample.shape
    depth, n_phys, page, n_heads, head_dim = cache_k.shape
    d_conv = conv_a_w.shape[-1]
    d_ff = w_down.shape[1]
    assert (n_heads, head_dim, ts) == (N_HEADS, HEAD_DIM, 1)
    assert sp % ROW_TILE == 0 and sp % ATTN_BLOCK == 0 and conv_a_w.shape[1] - 1 <= CONV_HALO

    qkv_end = 2 * d_conv + 3 * D_ATTN
    w_in_cols = jnp.concatenate(
        [w_in[..., :qkv_end], w_in[..., qkv_end + N_HEADS:], w_in[..., qkv_end:qkv_end + N_HEADS],
         jnp.zeros(w_in.shape[:2] + (LANES - N_HEADS,), w_in.dtype)], axis=-1).astype(BF16)
    lane_head = jnp.arange(D_ATTN) // HEAD_DIM
    row3 = lambda a: a[:, None, :]
    p = dict(
        d_conv=d_conv, w_in=w_in_cols, norm1_g=row3(norm1_g),
        b_f=row3(jnp.pad(b_f, ((0, 0), (0, LANES - N_HEADS)))),
        q_norm_g=row3(jnp.tile(q_norm_g, (1, N_HEADS))), k_norm_g=row3(jnp.tile(k_norm_g, (1, N_HEADS))),
        seg=(lane_head[:, None] == lane_head[None, :]).astype(BF16),
        conv_a_w=conv_a_w, conv_a_b=row3(conv_a_b), ln_a_g=row3(ln_a_g), ln_a_b=row3(ln_a_b),
        w_conv_out=w_conv_out.astype(BF16), w_attn_out=w_attn_out.astype(BF16), w_o=w_o.astype(BF16),
        norm2_g=row3(norm2_g), w_up=w_up.astype(BF16), conv_f_w=conv_f_w, conv_f_b=row3(conv_f_b),
        w_down=w_down.astype(BF16),
    )
    cache_k2 = jnp.transpose(cache_k, (0, 1, 3, 4, 2)).reshape(depth, n_phys, D_ATTN, page)
    cache_v2 = jnp.transpose(cache_v, (0, 1, 3, 4, 2)).reshape(depth, n_phys, D_ATTN, page)
    cache_ft = jnp.swapaxes(cache_logf, 2, 3)
    state_a = jnp.swapaxes(state_conv_a, 1, 2)
    state_f = jnp.swapaxes(state_conv_ffn, 1, 2)

    tiles_per_seq = sp // ROW_TILE
    xp = x_prompt.reshape(bp * sp, d_model)
    xs = x_sample.reshape(bs, d_model)
    outs = [[] for _ in range(10)]
    for layer in range(depth):
        u, q_aug, k, k_aug, v, vb, logf, ga, gb = _in_proj(xp, layer, p, ROW_TILE, tiles_per_seq=tiles_per_seq)
        o = _prompt_attention(q_aug.reshape(bp, sp, D_PAD), k_aug.reshape(bp, sp, D_PAD), vb.reshape(bp, sp, D_ATTN))
        x1 = _merge(u, o.reshape(bp * sp, D_ATTN), ga, gb, xp, layer, p, ROW_TILE, tiles_per_seq=tiles_per_seq)
        xp, gtail = _ffn(x1, layer, p, ROW_TILE, tiles_per_seq=tiles_per_seq)
        outs[0].append(k.reshape(bp, sp, N_HEADS, HEAD_DIM))
        outs[1].append(v.reshape(bp, sp, N_HEADS, HEAD_DIM))
        outs[2].append(logf.reshape(bp, sp, N_HEADS))
        outs[3].append(u.reshape(bp, sp, d_conv)[:, sp - (conv_a_w.shape[1] - 1):])
        outs[4].append(gtail.reshape(bp, tiles_per_seq, 8, d_ff)[:, -1, 6:])

        u, q, k, kb, v, vb, logf, ga, gb = _in_proj(xs, layer, p, bs)
        o = _decode_attention(q, k, v, logf, cache_k2, cache_v2, cache_ft, page_table, layer)
        x1 = _merge(u, o, ga, gb, xs, layer, p, bs, state=state_a)
        xs, g_new = _ffn(x1, layer, p, bs, state=state_f)
        outs[5].append(k.reshape(bs, ts, N_HEADS, HEAD_DIM))
        outs[6].append(v.reshape(bs, ts, N_HEADS, HEAD_DIM))
        outs[7].append(logf.reshape(bs, ts, N_HEADS))
        outs[8].append(jnp.concatenate([state_conv_a[layer][:, 1:], u[:, None, :]], axis=1))
        outs[9].append(jnp.concatenate([state_conv_ffn[layer][:, 1:], g_new[:, None, :]], axis=1))

    return (xp.reshape(bp, sp, d_model), xs.reshape(bs, ts, d_model)) + tuple(jnp.stack(o) for o in outs)
```

```python
import functools
import math

import numpy as np
import jax
import jax.numpy as jnp
from jax import lax
from jax.experimental import pallas as pl
from jax.experimental.pallas import tpu as pltpu

F32 = jnp.float32
BF16 = jnp.bfloat16

EPS = 1e-6
LOG2E = math.log2(math.e)
LANES = 128
SUBLANES = 8
HEAD_DIM = 64
N_HEADS = 8
D_ATTN = N_HEADS * HEAD_DIM
D_PAD = N_HEADS * LANES
CONV_HALO = 32
FFN_HALO = 16
ROW_TILE = 512
ATTN_BLOCK = 512
HEADS_PER_PASS = 8
ONES_ROWS = 16
CONV_ROWS = 64
FF_CHUNK = 512
MASKED = -1e30
VMEM_LIMIT_BYTES = 56 * 1024 * 1024


def _compiler_params(*semantics):
    return pltpu.CompilerParams(dimension_semantics=semantics, vmem_limit_bytes=VMEM_LIMIT_BYTES)


def _rms_scale(x, gain):
    return x * lax.rsqrt(jnp.mean(x * x, axis=-1, keepdims=True) + EPS) * gain


def _sigmoid(x):
    return 1.0 / (1.0 + jnp.exp(-x))


def _resident(shape, layer):
    zeros = (0,) * len(shape)
    return pl.BlockSpec((None,) + tuple(shape), lambda *_: (layer,) + zeros, pipeline_mode=pl.Buffered(1))


def _constant(shape):
    zeros = (0,) * len(shape)
    return pl.BlockSpec(tuple(shape), lambda *_: zeros, pipeline_mode=pl.Buffered(1))


def _split3(x):
    hi = x.astype(BF16)
    rest = x - hi.astype(F32)
    mid = rest.astype(BF16)
    return hi, mid, (rest - mid.astype(F32)).astype(BF16)


def _pad_heads(x, cols):
    low = lax.broadcasted_iota(jnp.int32, (1, LANES), 1) < HEAD_DIM
    tiles = []
    for pair in range(N_HEADS // 2):
        tile = x[:, pair * LANES:(pair + 1) * LANES]
        for h, src in ((2 * pair, tile), (2 * pair + 1, pltpu.roll(tile, HEAD_DIM, axis=1))):
            tiles.append(jnp.where(low, src, cols[:, h * LANES:(h + 1) * LANES]))
    return jnp.concatenate(tiles, axis=1)


def _in_proj_body(x_ref, g1_ref, wa_ref, wg_ref, wf_ref, bf_ref, qg_ref, kg_ref, seg_ref, *rest,
                  d_conv, d_model, tiles_per_seq):
    prompt = tiles_per_seq is not None
    if prompt:
        tri_ref, pq_ref, qc_ref, pk_ref, kc_ref = rest[:5]
        rest = rest[5:]
    u_ref, q_ref, k_ref, kb_ref, v_ref, vb_ref, logf_ref, ga_ref, gb_ref = rest[:9]
    tm = x_ref.shape[0]
    xn = _rms_scale(x_ref[...], g1_ref[...]).astype(BF16)

    def proj(w_ref, lo, width):
        return jnp.dot(xn, w_ref[:, lo:lo + width], preferred_element_type=F32)

    def head_norm(h, gain_ref):
        ms = jnp.dot((h * h).astype(BF16), seg_ref[...], preferred_element_type=F32) * (1.0 / HEAD_DIM)
        return h * lax.rsqrt(ms + EPS) * gain_ref[...]

    u_ref[...] = proj(wa_ref, 0, d_conv) * _sigmoid(proj(wa_ref, d_conv, d_conv))
    off = 2 * d_conv
    qn = head_norm(proj(wa_ref, off, D_ATTN), qg_ref)
    kn = head_norm(proj(wa_ref, off + D_ATTN, D_ATTN), kg_ref)
    v = proj(wa_ref, off + 2 * D_ATTN, D_ATTN)
    ga_ref[...] = _sigmoid(proj(wg_ref, 0, d_model)).astype(BF16)
    gb_ref[...] = _sigmoid(proj(wg_ref, d_model, d_model)).astype(BF16)
    z = proj(wf_ref, 0, LANES) + bf_ref[...]
    logf = jnp.minimum(z, 0.0) - jnp.log1p(jnp.exp(-jnp.abs(z)))
    logf_ref[...] = logf[:, :N_HEADS]

    if not prompt:
        q_ref[...] = (qn * HEAD_DIM ** -0.5).astype(BF16)
        k_ref[...] = kn
        kb_ref[...] = kn.astype(BF16)
        v_ref[...] = v
        vb_ref[...] = v.astype(BF16)
        return

    k_ref[0] = kn.T
    vt = v.T
    v_ref[0] = vt
    vb_ref[0] = vt.astype(BF16)

    carry_ref = rest[9]

    @pl.when(lax.rem(pl.program_id(0), tiles_per_seq) == 0)
    def _():
        carry_ref[...] = jnp.zeros_like(carry_ref)

    c = carry_ref[...]
    for piece in _split3(logf * LOG2E):
        c = c + jnp.dot(tri_ref[...], piece, preferred_element_type=F32)
    carry_ref[...] = c[tm - 1:tm, :]
    pieces = jnp.concatenate(_split3(c), axis=1)
    q_cols = jnp.dot(pieces, pq_ref[...], preferred_element_type=F32) + qc_ref[...]
    k_cols = jnp.dot(pieces, pk_ref[...], preferred_element_type=F32) + kc_ref[...]
    q_ref[0] = _pad_heads(qn * (LOG2E * HEAD_DIM ** -0.5), q_cols).T.astype(BF16)
    kb_ref[...] = _pad_heads(kn, k_cols).astype(BF16)


def _bias_placement():
    pq = np.zeros((3 * LANES, D_PAD), np.float32)
    pk = np.zeros((3 * LANES, D_PAD), np.float32)
    qc = np.zeros((1, D_PAD), np.float32)
    kc = np.zeros((1, D_PAD), np.float32)
    for h in range(N_HEADS):
        base = h * LANES + HEAD_DIM
        for j in range(3):
            pq[j * LANES + h, base + j] = 1.0
            kc[0, base + j] = 1.0
            pk[j * LANES + h, base + 3 + j] = -1.0
            qc[0, base + 3 + j] = 1.0
    return jnp.asarray(pq, BF16), jnp.asarray(qc), jnp.asarray(pk, BF16), jnp.asarray(kc)


def _in_proj(x, layer, p, tm, tiles_per_seq=None):
    t, d_model = x.shape
    d_conv = p["d_conv"]
    prompt = tiles_per_seq is not None
    row = lambda width: pl.BlockSpec((tm, width), lambda i: (i, 0))
    rows = lambda width, dtype: (row(width), jax.ShapeDtypeStruct((t, width), dtype))
    in_specs = [row(d_model), _resident((1, d_model), layer)]
    in_specs += [_resident((d_model, p[name].shape[-1]), layer) for name in ("w_in_a", "w_in_g", "w_in_f")]
    in_specs += [_resident((1, LANES), layer), _resident((1, D_ATTN), layer), _resident((1, D_ATTN), layer),
                 _constant((D_ATTN, D_ATTN))]
    args = [x, p["norm1_g"], p["w_in_a"], p["w_in_g"], p["w_in_f"], p["b_f"], p["q_norm_g"], p["k_norm_g"], p["seg"]]
    scratch = []
    if prompt:
        tri = jnp.asarray(np.tril(np.ones((tm, tm), np.float32)), BF16)
        consts = (tri,) + _bias_placement()
        in_specs += [_constant(c.shape) for c in consts]
        args += list(consts)
        scratch = [pltpu.VMEM((1, LANES), F32)]
        batch = t // (tiles_per_seq * tm)
        cols = lambda width, dtype: (
            pl.BlockSpec((1, width, tm), lambda i: (i // tiles_per_seq, 0, lax.rem(i, tiles_per_seq))),
            jax.ShapeDtypeStruct((batch, width, tiles_per_seq * tm), dtype))
        qkv = [cols(D_PAD, BF16), cols(D_ATTN, F32), rows(D_PAD, BF16), cols(D_ATTN, F32), cols(D_ATTN, BF16)]
    else:
        qkv = [rows(D_ATTN, BF16), rows(D_ATTN, F32), rows(D_ATTN, BF16), rows(D_ATTN, F32), rows(D_ATTN, BF16)]
    outs = [rows(d_conv, F32)] + qkv + [rows(N_HEADS, F32), rows(d_model, BF16), rows(d_model, BF16)]
    return pl.pallas_call(
        functools.partial(_in_proj_body, d_conv=d_conv, d_model=d_model, tiles_per_seq=tiles_per_seq),
        grid=(t // tm,),
        in_specs=in_specs,
        out_specs=[spec for spec, _ in outs],
        out_shape=[shape for _, shape in outs],
        scratch_shapes=scratch,
        compiler_params=_compiler_params("arbitrary"),
        name="in_proj",
    )(*args)


def _flash_body(qt_ref, k_ref, vt_ref, o_ref, *, blk):
    i = pl.program_id(1)
    causal = (lax.broadcasted_iota(jnp.int32, (blk, blk), 0) <= lax.broadcasted_iota(jnp.int32, (blk, blk), 1))
    ones_rows = jnp.ones((ONES_ROWS, blk), BF16)
    normed = []
    for group in range(N_HEADS // HEADS_PER_PASS):
        heads = tuple(range(group * HEADS_PER_PASS, (group + 1) * HEADS_PER_PASS))
        q_heads = [qt_ref[0, h * LANES:(h + 1) * LANES, :] for h in heads]

        def block(j, carry, masked, heads=heads, q_heads=q_heads):
            start = pl.multiple_of(j * blk, blk)
            out = []
            for idx, h in enumerate(heads):
                m, acc = carry[2 * idx:2 * idx + 2]
                st = jnp.dot(k_ref[0, pl.ds(start, blk), h * LANES:(h + 1) * LANES], q_heads[idx],
                             preferred_element_type=F32)
                if masked:
                    st = jnp.where(causal, st, MASKED)
                m_new = jnp.maximum(m, jnp.max(st, axis=0, keepdims=True))
                pr = jnp.exp2(st - m_new).astype(BF16)
                vt = vt_ref[0, h * HEAD_DIM:(h + 1) * HEAD_DIM, pl.ds(start, blk)]
                pv = jnp.dot(jnp.concatenate([vt, ones_rows], axis=0), pr, preferred_element_type=F32)
                out += [m_new, jnp.exp2(m - m_new) * acc + pv]
            return tuple(out)

        init = (jnp.full((1, blk), MASKED, F32), jnp.zeros((HEAD_DIM + ONES_ROWS, blk), F32)) * len(heads)
        carry = lax.fori_loop(0, i, functools.partial(block, masked=False), init)
        carry = block(i, carry, masked=True)
        for idx in range(len(heads)):
            acc = carry[2 * idx + 1]
            normed.append(acc[:HEAD_DIM] / acc[HEAD_DIM:HEAD_DIM + 1])
    o_ref[0] = jnp.concatenate(normed, axis=0).T.astype(BF16)


def _prompt_attention(qt, k_aug, vt):
    b, d, s = vt.shape
    blk = ATTN_BLOCK
    return pl.pallas_call(
        functools.partial(_flash_body, blk=blk),
        grid=(b, s // blk),
        in_specs=[pl.BlockSpec((1, D_PAD, blk), lambda bi, i: (bi, 0, i)),
                  pl.BlockSpec((1, s, D_PAD), lambda bi, i: (bi, 0, 0)),
                  pl.BlockSpec((1, d, s), lambda bi, i: (bi, 0, 0))],
        out_specs=pl.BlockSpec((1, blk, d), lambda bi, i: (bi, i, 0)),
        out_shape=jax.ShapeDtypeStruct((b, s, d), BF16),
        compiler_params=_compiler_params("parallel", "arbitrary"),
        name="prompt_attention",
    )(qt, k_aug, vt)


def _decode_body(pt_ref, q_ref, kn_ref, vn_ref, fn_ref, ck_hbm, cv_hbm, cf_hbm, o_ref,
                 kbuf, vbuf, fbuf, sem, *, layer, n_pages, page):
    b = pl.program_id(0)
    nb = pl.num_programs(0)
    slot = lax.rem(b, 2)
    past = n_pages * page

    def page_copies(seq, slot):
        copies = []
        for pg in range(n_pages):
            phys = pt_ref[seq * n_pages + pg]
            cols = pl.ds(pg * page, page)
            copies.append(pltpu.make_async_copy(ck_hbm.at[layer, phys], kbuf.at[slot, :, cols], sem.at[0, slot]))
            copies.append(pltpu.make_async_copy(cv_hbm.at[layer, phys], vbuf.at[slot, :, cols], sem.at[1, slot]))
            copies.append(pltpu.make_async_copy(cf_hbm.at[layer, phys], fbuf.at[slot, :, cols], sem.at[2, slot]))
        return copies

    @pl.when(b == 0)
    def _():
        for cp in page_copies(0, 0):
            cp.start()

    @pl.when(b + 1 < nb)
    def _():
        for cp in page_copies(b + 1, 1 - slot):
            cp.start()

    for cp in page_copies(b, slot):
        cp.wait()

    d = q_ref.shape[-1]
    head_of_lane = lax.shift_right_logical(lax.broadcasted_iota(jnp.int32, (N_HEADS, d), 1), 6)
    own = head_of_lane == lax.broadcasted_iota(jnp.int32, (N_HEADS, d), 0)
    qbd = jnp.where(own, q_ref[0].astype(F32), 0.0)
    s = jnp.dot(qbd.astype(BF16), kbuf[slot].astype(BF16), preferred_element_type=F32)

    f = fbuf[slot]
    lane = lax.broadcasted_iota(jnp.int32, f.shape, 1)
    x = jnp.where(lane < past - 1, pltpu.roll(f, past - 1, axis=1), 0.0)
    shift = 1
    while shift < past:
        x = x + jnp.where(lane < past - shift, pltpu.roll(x, past - shift, axis=1), 0.0)
        shift *= 2
    s = s + (x + fn_ref[0])

    s_new = jnp.sum(qbd * kn_ref[0], axis=1, keepdims=True)
    m = jnp.maximum(jnp.max(s, axis=1, keepdims=True), s_new)
    pr = jnp.exp(s - m)
    p_new = jnp.exp(s_new - m)
    denom = jnp.sum(pr, axis=1, keepdims=True) + p_new
    o8 = lax.dot_general(pr.astype(BF16), vbuf[slot].astype(BF16), (((1,), (1,)), ((), ())),
                         preferred_element_type=F32) + p_new * vn_ref[0]
    o_ref[0] = jnp.sum(jnp.where(own, o8 / denom, 0.0), axis=0, keepdims=True).astype(BF16)


def _decode_attention(q, k_new, v_new, f_new, cache_k, cache_v, cache_ft, page_table, layer):
    bs, d = q.shape
    n_pages = page_table.shape[1]
    page = cache_k.shape[3]
    past = n_pages * page
    per_seq = lambda shape: pl.BlockSpec((1,) + shape, lambda b, pt: (b, 0, 0))
    any_space = pl.BlockSpec(memory_space=pl.ANY)
    grid_spec = pltpu.PrefetchScalarGridSpec(
        num_scalar_prefetch=1,
        grid=(bs,),
        in_specs=[per_seq((1, d)), per_seq((1, d)), per_seq((1, d)), per_seq((N_HEADS, 1)),
                  any_space, any_space, any_space],
        out_specs=per_seq((1, d)),
        scratch_shapes=[pltpu.VMEM((2, d, past), F32), pltpu.VMEM((2, d, past), F32),
                        pltpu.VMEM((2, N_HEADS, past), F32), pltpu.SemaphoreType.DMA((3, 2))],
    )
    out = pl.pallas_call(
        functools.partial(_decode_body, layer=layer, n_pages=n_pages, page=page),
        grid_spec=grid_spec,
        out_shape=jax.ShapeDtypeStruct((bs, 1, d), BF16),
        compiler_params=_compiler_params("arbitrary"),
        name="decode_attention",
    )(page_table.reshape(-1), q.reshape(bs, 1, d), k_new.reshape(bs, 1, d), v_new.reshape(bs, 1, d),
      f_new.reshape(bs, N_HEADS, 1), cache_k, cache_v, cache_ft)
    return out.reshape(bs, d)


def _merge_tail(conv, o_ref, ga_ref, gb_ref, x_ref, lg_ref, lb_ref, wc_ref, wa_ref, wo_ref, out_ref):
    mu = jnp.mean(conv, axis=-1, keepdims=True)
    dev = conv - mu
    y = dev * lax.rsqrt(jnp.mean(dev * dev, axis=-1, keepdims=True) + EPS) * lg_ref[...] + lb_ref[...]
    a_out = jnp.dot((y * _sigmoid(y)).astype(BF16), wc_ref[...], preferred_element_type=F32)
    b_out = jnp.dot(o_ref[...], wa_ref[...], preferred_element_type=F32)
    mix = ga_ref[...].astype(F32) * a_out + gb_ref[...].astype(F32) * b_out
    out_ref[...] = x_ref[...] + jnp.dot(mix.astype(BF16), wo_ref[...], preferred_element_type=F32)


def _merge_prompt_body(u_ref, halo_ref, o_ref, ga_ref, gb_ref, x_ref, cw_ref, cb_ref, lg_ref, lb_ref,
                       wc_ref, wa_ref, wo_ref, out_ref, win_ref, conv_ref, *, tiles_per_seq, width):
    tm = u_ref.shape[0]
    first = lax.rem(pl.program_id(0), tiles_per_seq) == 0
    win_ref[0, 0:CONV_HALO, :] = jnp.where(first, 0.0, halo_ref[...])
    win_ref[0, CONV_HALO:, :] = u_ref[...]
    span = tm + CONV_HALO - SUBLANES
    for r in range(1, SUBLANES):
        win_ref[r, 0:span, :] = win_ref[0, r:r + span, :]
    lead = CONV_HALO - (width - 1)
    for r0 in range(0, tm, CONV_ROWS):
        acc = jnp.broadcast_to(cb_ref[...], (CONV_ROWS, cb_ref.shape[-1]))
        for j in range(width):
            r = (lead + j) % SUBLANES
            base = r0 + lead + j - r
            acc = acc + cw_ref[j:j + 1, :] * win_ref[r, base:base + CONV_ROWS, :]
        conv_ref[r0:r0 + CONV_ROWS, :] = acc
    _merge_tail(conv_ref[...], o_ref, ga_ref, gb_ref, x_ref, lg_ref, lb_ref, wc_ref, wa_ref, wo_ref, out_ref)


def _merge_sample_body(u_ref, st_ref, o_ref, ga_ref, gb_ref, x_ref, cw_ref, cb_ref, lg_ref, lb_ref,
                       wc_ref, wa_ref, wo_ref, out_ref, new_st_ref, *, width):
    u = u_ref[...]
    conv = cb_ref[...] + cw_ref[width - 1:width, :] * u
    for j in range(width - 1):
        conv = conv + cw_ref[j:j + 1, :] * st_ref[j]
    new_st_ref[0:width - 2] = st_ref[1:width - 1]
    new_st_ref[width - 2] = u
    _merge_tail(conv, o_ref, ga_ref, gb_ref, x_ref, lg_ref, lb_ref, wc_ref, wa_ref, wo_ref, out_ref)


def _merge(u, o, ga, gb, x, layer, p, tm, tiles_per_seq=None, state=None):
    t, d_model = x.shape
    d_conv = u.shape[1]
    width = p["conv_a_w"].shape[1]
    row = lambda w: pl.BlockSpec((tm, w), lambda i: (i, 0))
    weights = [_resident((width, d_conv), layer), _resident((1, d_conv), layer), _resident((1, d_conv), layer),
               _resident((1, d_conv), layer), _resident((d_conv, d_model), layer),
               _resident((D_ATTN, d_model), layer), _resident((d_model, d_model), layer)]
    weight_args = (p["conv_a_w"], p["conv_a_b"], p["ln_a_g"], p["ln_a_b"], p["w_conv_out"], p["w_attn_out"], p["w_o"])
    rows = [row(D_ATTN), row(d_model), row(d_model), row(d_model)]
    if state is None:
        halo_blocks = tm // CONV_HALO
        halo = pl.BlockSpec((CONV_HALO, d_conv), lambda i: (jnp.maximum(i * halo_blocks - 1, 0), 0))
        body = functools.partial(_merge_prompt_body, tiles_per_seq=tiles_per_seq, width=width)
        in_specs = [row(d_conv), halo] + rows + weights
        args = (u, u, o, ga, gb, x) + weight_args
        scratch = [pltpu.VMEM((SUBLANES, tm + CONV_HALO, d_conv), F32), pltpu.VMEM((tm, d_conv), F32)]
        out_specs = [row(d_model)]
        out_shape = [jax.ShapeDtypeStruct((t, d_model), F32)]
    else:
        body = functools.partial(_merge_sample_body, width=width)
        state_spec = pl.BlockSpec((None, width - 1, tm, d_conv), lambda i: (layer, 0, i, 0))
        in_specs = [row(d_conv), state_spec] + rows + weights
        args = (u, state, o, ga, gb, x) + weight_args
        scratch = []
        out_specs = [row(d_model), pl.BlockSpec((width - 1, tm, d_conv), lambda i: (0, i, 0))]
        out_shape = [jax.ShapeDtypeStruct((t, d_model), F32), jax.ShapeDtypeStruct((width - 1, t, d_conv), F32)]
    return pl.pallas_call(
        body, grid=(t // tm,), in_specs=in_specs, out_specs=out_specs, out_shape=out_shape, scratch_shapes=scratch,
        compiler_params=_compiler_params("parallel"), name="merge",
    )(*args)


def _ff_chunks(d_ff):
    return [(c0, min(c0 + FF_CHUNK, d_ff)) for c0 in range(0, d_ff, FF_CHUNK)]


def _ffn_prompt_body(x_ref, halo_ref, g2_ref, wup_ref, fw_ref, fb_ref, wdn_ref, out_ref, gtail_ref, gext_ref,
                     hidden_ref, *, tiles_per_seq, d_ff):
    tm = x_ref.shape[0]
    first = lax.rem(pl.program_id(0), tiles_per_seq) == 0
    x = x_ref[...]
    xn = _rms_scale(x, g2_ref[...]).astype(BF16)
    xn_halo = _rms_scale(jnp.where(first, 0.0, halo_ref[...]), g2_ref[...]).astype(BF16)
    xn_ext = jnp.concatenate([xn_halo, xn], axis=0)
    for c0, c1 in _ff_chunks(d_ff):
        gext_ref[:, c0:c1] = jnp.dot(xn_ext, wup_ref[:, c0:c1], preferred_element_type=F32)
        up = jnp.dot(xn, wup_ref[:, d_ff + c0:d_ff + c1], preferred_element_type=F32)
        gc = fb_ref[:, c0:c1]
        for j in range(3):
            lo = FFN_HALO - 2 + j
            gc = gc + fw_ref[j:j + 1, c0:c1] * gext_ref[lo:lo + tm, c0:c1]
        hidden_ref[:, c0:c1] = (gc * _sigmoid(gc) * up).astype(BF16)
        gtail_ref[:, c0:c1] = gext_ref[FFN_HALO + tm - 8:FFN_HALO + tm, c0:c1]
    out_ref[...] = x + jnp.dot(hidden_ref[...], wdn_ref[...], preferred_element_type=F32)


def _ffn_sample_body(x_ref, st_ref, g2_ref, wup_ref, fw_ref, fb_ref, wdn_ref, out_ref, g_ref, *, d_ff):
    x = x_ref[...]
    xn = _rms_scale(x, g2_ref[...]).astype(BF16)
    acc = x
    for c0, c1 in _ff_chunks(d_ff):
        g = jnp.dot(xn, wup_ref[:, c0:c1], preferred_element_type=F32)
        up = jnp.dot(xn, wup_ref[:, d_ff + c0:d_ff + c1], preferred_element_type=F32)
        gc = (fb_ref[:, c0:c1] + fw_ref[0:1, c0:c1] * st_ref[0, :, c0:c1]
              + fw_ref[1:2, c0:c1] * st_ref[1, :, c0:c1] + fw_ref[2:3, c0:c1] * g)
        hidden = (gc * _sigmoid(gc) * up).astype(BF16)
        acc = acc + jnp.dot(hidden, wdn_ref[c0:c1, :], preferred_element_type=F32)
        g_ref[:, c0:c1] = g
    out_ref[...] = acc


def _ffn(x, layer, p, tm, tiles_per_seq=None, state=None):
    t, d_model = x.shape
    d_ff = p["w_down"].shape[1]
    row = pl.BlockSpec((tm, d_model), lambda i: (i, 0))
    weights = [_resident((1, d_model), layer), _resident((d_model, 2 * d_ff), layer), _resident((3, d_ff), layer),
               _resident((1, d_ff), layer), _resident((d_ff, d_model), layer)]
    weight_args = (p["norm2_g"], p["w_up"], p["conv_f_w"], p["conv_f_b"], p["w_down"])
    if state is None:
        halo_blocks = tm // FFN_HALO
        halo = pl.BlockSpec((FFN_HALO, d_model), lambda i: (jnp.maximum(i * halo_blocks - 1, 0), 0))
        body = functools.partial(_ffn_prompt_body, tiles_per_seq=tiles_per_seq, d_ff=d_ff)
        in_specs = [row, halo] + weights
        args = (x, x) + weight_args
        out_specs = [row, pl.BlockSpec((8, d_ff), lambda i: (i, 0))]
        out_shape = [jax.ShapeDtypeStruct((t, d_model), F32), jax.ShapeDtypeStruct((t // tm * 8, d_ff), F32)]
        scratch = [pltpu.VMEM((tm + FFN_HALO, d_ff), F32), pltpu.VMEM((tm, d_ff), BF16)]
    else:
        body = functools.partial(_ffn_sample_body, d_ff=d_ff)
        in_specs = [row, pl.BlockSpec((None, 2, tm, d_ff), lambda i: (layer, 0, i, 0))] + weights
        args = (x, state) + weight_args
        out_specs = [row, pl.BlockSpec((tm, d_ff), lambda i: (i, 0))]
        out_shape = [jax.ShapeDtypeStruct((t, d_model), F32), jax.ShapeDtypeStruct((t, d_ff), F32)]
        scratch = []
    return pl.pallas_call(
        body, grid=(t // tm,), in_specs=in_specs, out_specs=out_specs, out_shape=out_shape, scratch_shapes=scratch,
        compiler_params=_compiler_params("parallel"), name="conv_ffn",
    )(*args)


def kernel(x_prompt, x_sample, cache_k, cache_v, cache_logf, state_conv_a, state_conv_ffn, page_table, norm1_g, w_in, b_f, q_norm_g, k_norm_g, conv_a_w, conv_a_b, ln_a_g, ln_a_b, w_conv_out, w_attn_out, w_o, norm2_g, w_up, conv_f_w, conv_f_b, w_down):
    bp, sp, d_model = x_prompt.shape
    bs, ts, _ = x_sample.shape
    depth, n_phys, page, n_heads, head_dim = cache_k.shape
    d_conv = conv_a_w.shape[-1]
    d_ff = w_down.shape[1]
    assert (n_heads, head_dim, ts) == (N_HEADS, HEAD_DIM, 1)
    assert sp % ROW_TILE == 0 and sp % ATTN_BLOCK == 0 and conv_a_w.shape[1] - 1 <= CONV_HALO

    qkv_end = 2 * d_conv + 3 * D_ATTN
    lane_head = jnp.arange(D_ATTN) // HEAD_DIM
    row3 = lambda a: a[:, None, :]
    p = dict(
        d_conv=d_conv, w_in_a=w_in[..., :qkv_end].astype(BF16), w_in_g=w_in[..., qkv_end + N_HEADS:].astype(BF16),
        w_in_f=jnp.pad(w_in[..., qkv_end:qkv_end + N_HEADS], ((0, 0), (0, 0), (0, LANES - N_HEADS))).astype(BF16),
        norm1_g=row3(norm1_g),
        b_f=row3(jnp.pad(b_f, ((0, 0), (0, LANES - N_HEADS)))),
        q_norm_g=row3(jnp.tile(q_norm_g, (1, N_HEADS))), k_norm_g=row3(jnp.tile(k_norm_g, (1, N_HEADS))),
        seg=(lane_head[:, None] == lane_head[None, :]).astype(BF16),
        conv_a_w=conv_a_w, conv_a_b=row3(conv_a_b), ln_a_g=row3(ln_a_g), ln_a_b=row3(ln_a_b),
        w_conv_out=w_conv_out.astype(BF16), w_attn_out=w_attn_out.astype(BF16), w_o=w_o.astype(BF16),
        norm2_g=row3(norm2_g), w_up=w_up.astype(BF16), conv_f_w=conv_f_w, conv_f_b=row3(conv_f_b),
        w_down=w_down.astype(BF16),
    )
    cache_k2 = jnp.transpose(cache_k, (0, 1, 3, 4, 2)).reshape(depth, n_phys, D_ATTN, page)
    cache_v2 = jnp.transpose(cache_v, (0, 1, 3, 4, 2)).reshape(depth, n_phys, D_ATTN, page)
    cache_ft = jnp.swapaxes(cache_logf, 2, 3)
    state_a = jnp.swapaxes(state_conv_a, 1, 2)
    state_f = jnp.swapaxes(state_conv_ffn, 1, 2)

    tiles_per_seq = sp // ROW_TILE
    xp = x_prompt.reshape(bp * sp, d_model)
    xs = x_sample.reshape(bs, d_model)
    outs = [[] for _ in range(10)]
    for layer in range(depth):
        u, qt, kt, k_aug, vt, vtb, logf, ga, gb = _in_proj(xp, layer, p, ROW_TILE, tiles_per_seq=tiles_per_seq)
        o = _prompt_attention(qt, k_aug.reshape(bp, sp, D_PAD), vtb)
        x1, = _merge(u, o.reshape(bp * sp, D_ATTN), ga, gb, xp, layer, p, ROW_TILE, tiles_per_seq=tiles_per_seq)
        xp, gtail = _ffn(x1, layer, p, ROW_TILE, tiles_per_seq=tiles_per_seq)
        outs[0].append(kt)
        outs[1].append(vt)
        outs[2].append(logf.reshape(bp, sp, N_HEADS))
        outs[3].append(u.reshape(bp, sp, d_conv)[:, sp - (conv_a_w.shape[1] - 1):])
        outs[4].append(gtail.reshape(bp, tiles_per_seq, 8, d_ff)[:, -1, 6:])

        u, q, k, kb, v, vb, logf, ga, gb = _in_proj(xs, layer, p, bs)
        o = _decode_attention(q, k, v, logf, cache_k2, cache_v2, cache_ft, page_table, layer)
        x1, new_state_a = _merge(u, o, ga, gb, xs, layer, p, bs, state=state_a)
        xs, g_new = _ffn(x1, layer, p, bs, state=state_f)
        outs[5].append(k.reshape(bs, ts, N_HEADS, HEAD_DIM))
        outs[6].append(v.reshape(bs, ts, N_HEADS, HEAD_DIM))
        outs[7].append(logf.reshape(bs, ts, N_HEADS))
        outs[8].append(new_state_a)
        outs[9].append(jnp.concatenate([state_conv_ffn[layer][:, 1:], g_new[:, None, :]], axis=1))

    outs = [jnp.stack(o) for o in outs]
    for idx in (0, 1):
        outs[idx] = jnp.transpose(outs[idx].reshape(depth, bp, N_HEADS, HEAD_DIM, sp), (0, 1, 4, 2, 3))
    outs[8] = jnp.swapaxes(outs[8], 1, 2)
    return (xp.reshape(bp, sp, d_model), xs.reshape(bs, ts, d_model)) + tuple(outs)
```

```python
import functools
import math

import numpy as np
import jax
import jax.numpy as jnp
from jax import lax
from jax.experimental import pallas as pl
from jax.experimental.pallas import tpu as pltpu

F32 = jnp.float32
BF16 = jnp.bfloat16

EPS = 1e-6
LOG2E = math.log2(math.e)
LANES = 128
SUBLANES = 8
HEAD_DIM = 64
N_HEADS = 8
D_ATTN = N_HEADS * HEAD_DIM
D_PAD = N_HEADS * LANES
CONV_HALO = 32
FFN_HALO = 16
ROW_TILE = 512
ATTN_BLOCK = 512
HEADS_PER_PASS = 8
ONES_ROWS = 16
CONV_ROWS = 64
FF_CHUNK = 256
MASKED = -1e30
VMEM_LIMIT_BYTES = 56 * 1024 * 1024


def _compiler_params(*semantics):
    return pltpu.CompilerParams(dimension_semantics=semantics, vmem_limit_bytes=VMEM_LIMIT_BYTES)


def _rms_scale(x, gain):
    return x * lax.rsqrt(jnp.mean(x * x, axis=-1, keepdims=True) + EPS) * gain


def _sigmoid(x):
    return 1.0 / (1.0 + jnp.exp(-x))


def _resident(shape, layer):
    zeros = (0,) * len(shape)
    return pl.BlockSpec((None,) + tuple(shape), lambda *_: (layer,) + zeros, pipeline_mode=pl.Buffered(1))


def _constant(shape):
    zeros = (0,) * len(shape)
    return pl.BlockSpec(tuple(shape), lambda *_: zeros, pipeline_mode=pl.Buffered(1))


def _split3(x):
    hi = x.astype(BF16)
    rest = x - hi.astype(F32)
    mid = rest.astype(BF16)
    return hi, mid, (rest - mid.astype(F32)).astype(BF16)


def _pad_heads(x, cols):
    low = lax.broadcasted_iota(jnp.int32, (1, LANES), 1) < HEAD_DIM
    tiles = []
    for pair in range(N_HEADS // 2):
        tile = x[:, pair * LANES:(pair + 1) * LANES]
        for h, src in ((2 * pair, tile), (2 * pair + 1, pltpu.roll(tile, HEAD_DIM, axis=1))):
            tiles.append(jnp.where(low, src, cols[:, h * LANES:(h + 1) * LANES]))
    return jnp.concatenate(tiles, axis=1)


def _in_proj_body(x_ref, g1_ref, wa_ref, wg_ref, wf_ref, bf_ref, qg_ref, kg_ref, seg_ref, *rest,
                  d_conv, d_model, tiles_per_seq):
    prompt = tiles_per_seq is not None
    if prompt:
        tri_ref, pq_ref, qc_ref, pk_ref, kc_ref = rest[:5]
        rest = rest[7:]
    u_ref, q_ref, k_ref, kb_ref, v_ref, vb_ref, logf_ref, ga_ref, gb_ref = rest[:9]
    tm = x_ref.shape[0]
    xn = _rms_scale(x_ref[...], g1_ref[...]).astype(BF16)

    def proj(w_ref, lo, width):
        return jnp.dot(xn, w_ref[:, lo:lo + width], preferred_element_type=F32)

    def head_norm(h, gain_ref):
        ms = jnp.dot((h * h).astype(BF16), seg_ref[...], preferred_element_type=F32) * (1.0 / HEAD_DIM)
        return h * lax.rsqrt(ms + EPS) * gain_ref[...]

    u_ref[...] = proj(wa_ref, 0, d_conv) * _sigmoid(proj(wa_ref, d_conv, d_conv))
    off = 2 * d_conv
    qn = head_norm(proj(wa_ref, off, D_ATTN), qg_ref)
    kn = head_norm(proj(wa_ref, off + D_ATTN, D_ATTN), kg_ref)
    v = proj(wa_ref, off + 2 * D_ATTN, D_ATTN)
    ga_ref[...] = _sigmoid(proj(wg_ref, 0, d_model)).astype(BF16)
    gb_ref[...] = _sigmoid(proj(wg_ref, d_model, d_model)).astype(BF16)
    z = proj(wf_ref, 0, LANES) + bf_ref[...]
    logf = jnp.minimum(z, 0.0) - jnp.log1p(jnp.exp(-jnp.abs(z)))
    logf_ref[...] = logf[:, :N_HEADS]

    if not prompt:
        q_ref[...] = (qn * HEAD_DIM ** -0.5).astype(BF16)
        k_ref[...] = kn
        kb_ref[...] = kn.astype(BF16)
        v_ref[...] = v
        vb_ref[...] = v.astype(BF16)
        return

    k_ref[0] = kn.T
    vt = v.T
    v_ref[0] = vt
    vb_ref[0] = vt.astype(BF16)

    carry_ref = rest[9]

    @pl.when(lax.rem(pl.program_id(0), tiles_per_seq) == 0)
    def _():
        carry_ref[...] = jnp.zeros_like(carry_ref)

    c = carry_ref[...]
    for piece in _split3(logf * LOG2E):
        c = c + jnp.dot(tri_ref[...], piece, preferred_element_type=F32)
    carry_ref[...] = c[tm - 1:tm, :]
    pieces = jnp.concatenate(_split3(c), axis=1)
    q_cols = jnp.dot(pieces, pq_ref[...], preferred_element_type=F32) + qc_ref[...]
    k_cols = jnp.dot(pieces, pk_ref[...], preferred_element_type=F32) + kc_ref[...]
    q_ref[0] = _pad_heads(qn * (LOG2E * HEAD_DIM ** -0.5), q_cols).T.astype(BF16)
    kb_ref[...] = _pad_heads(kn, k_cols).astype(BF16)


def _bias_placement():
    pq = np.zeros((3 * LANES, D_PAD), np.float32)
    pk = np.zeros((3 * LANES, D_PAD), np.float32)
    qc = np.zeros((1, D_PAD), np.float32)
    kc = np.zeros((1, D_PAD), np.float32)
    for h in range(N_HEADS):
        base = h * LANES + HEAD_DIM
        for j in range(3):
            pq[j * LANES + h, base + j] = 1.0
            kc[0, base + j] = 1.0
            pk[j * LANES + h, base + 3 + j] = -1.0
            qc[0, base + 3 + j] = 1.0
    return jnp.asarray(pq, BF16), jnp.asarray(qc), jnp.asarray(pk, BF16), jnp.asarray(kc)


def _in_proj(x, layer, p, tm, tiles_per_seq=None, kv_all=None):
    t, d_model = x.shape
    d_conv = p["d_conv"]
    prompt = tiles_per_seq is not None
    row = lambda width: pl.BlockSpec((tm, width), lambda i: (i, 0))
    rows = lambda width, dtype: (row(width), jax.ShapeDtypeStruct((t, width), dtype))
    in_specs = [row(d_model), _resident((1, d_model), layer)]
    in_specs += [_resident((d_model, p[name].shape[-1]), layer) for name in ("w_in_a", "w_in_g", "w_in_f")]
    in_specs += [_resident((1, LANES), layer), _resident((1, D_ATTN), layer), _resident((1, D_ATTN), layer),
                 _constant((D_ATTN, D_ATTN))]
    args = [x, p["norm1_g"], p["w_in_a"], p["w_in_g"], p["w_in_f"], p["b_f"], p["q_norm_g"], p["k_norm_g"], p["seg"]]
    scratch = []
    if prompt:
        tri = jnp.asarray(np.tril(np.ones((tm, tm), np.float32)), BF16)
        consts = (tri,) + _bias_placement()
        in_specs += [_constant(c.shape) for c in consts]
        args += list(consts)
        scratch = [pltpu.VMEM((1, LANES), F32)]
        batch = t // (tiles_per_seq * tm)
        cols = lambda width, dtype: (
            pl.BlockSpec((1, width, tm), lambda i: (i // tiles_per_seq, 0, lax.rem(i, tiles_per_seq))),
            jax.ShapeDtypeStruct((batch, width, tiles_per_seq * tm), dtype))
        layer_cols = lambda arr: (
            pl.BlockSpec((None, 1, D_ATTN, tm), lambda i: (layer, i // tiles_per_seq, 0, lax.rem(i, tiles_per_seq))),
            jax.ShapeDtypeStruct(arr.shape, arr.dtype))
        qkv = [cols(D_PAD, BF16), layer_cols(kv_all[0]), rows(D_PAD, BF16), layer_cols(kv_all[1]), cols(D_ATTN, BF16)]
        aliases = {len(args): 2, len(args) + 1: 4}
        in_specs += [pl.BlockSpec(memory_space=pl.ANY)] * 2
        args += list(kv_all)
    else:
        qkv = [rows(D_ATTN, BF16), rows(D_ATTN, F32), rows(D_ATTN, BF16), rows(D_ATTN, F32), rows(D_ATTN, BF16)]
        aliases = {}
    outs = [rows(d_conv, F32)] + qkv + [rows(N_HEADS, F32), rows(d_model, BF16), rows(d_model, BF16)]
    return pl.pallas_call(
        functools.partial(_in_proj_body, d_conv=d_conv, d_model=d_model, tiles_per_seq=tiles_per_seq),
        grid=(t // tm,),
        in_specs=in_specs,
        out_specs=[spec for spec, _ in outs],
        out_shape=[shape for _, shape in outs],
        input_output_aliases=aliases,
        scratch_shapes=scratch,
        compiler_params=_compiler_params("arbitrary"),
        name="in_proj",
    )(*args)


def _flash_body(qt_ref, k_ref, vt_ref, o_ref, *, blk):
    i = pl.program_id(1)
    causal = (lax.broadcasted_iota(jnp.int32, (blk, blk), 0) <= lax.broadcasted_iota(jnp.int32, (blk, blk), 1))
    ones_rows = jnp.ones((ONES_ROWS, blk), BF16)
    normed = []
    for group in range(N_HEADS // HEADS_PER_PASS):
        heads = tuple(range(group * HEADS_PER_PASS, (group + 1) * HEADS_PER_PASS))
        q_heads = [qt_ref[0, h * LANES:(h + 1) * LANES, :] for h in heads]

        def block(j, carry, masked, heads=heads, q_heads=q_heads):
            start = pl.multiple_of(j * blk, blk)
            out = []
            for idx, h in enumerate(heads):
                m, acc = carry[2 * idx:2 * idx + 2]
                st = jnp.dot(k_ref[0, pl.ds(start, blk), h * LANES:(h + 1) * LANES], q_heads[idx],
                             preferred_element_type=F32)
                if masked:
                    st = jnp.where(causal, st, MASKED)
                m_new = jnp.maximum(m, jnp.max(st, axis=0, keepdims=True))
                pr = jnp.exp2(st - m_new).astype(BF16)
                vt = vt_ref[0, h * HEAD_DIM:(h + 1) * HEAD_DIM, pl.ds(start, blk)]
                pv = jnp.dot(jnp.concatenate([vt, ones_rows], axis=0), pr, preferred_element_type=F32)
                out += [m_new, jnp.exp2(m - m_new) * acc + pv]
            return tuple(out)

        init = (jnp.full((1, blk), MASKED, F32), jnp.zeros((HEAD_DIM + ONES_ROWS, blk), F32)) * len(heads)
        carry = lax.fori_loop(0, i, functools.partial(block, masked=False), init)
        carry = block(i, carry, masked=True)
        for idx in range(len(heads)):
            acc = carry[2 * idx + 1]
            normed.append(acc[:HEAD_DIM] / acc[HEAD_DIM:HEAD_DIM + 1])
    o_ref[0] = jnp.concatenate(normed, axis=0).T.astype(BF16)


def _prompt_attention(qt, k_aug, vt):
    b, d, s = vt.shape
    blk = ATTN_BLOCK
    return pl.pallas_call(
        functools.partial(_flash_body, blk=blk),
        grid=(b, s // blk),
        in_specs=[pl.BlockSpec((1, D_PAD, blk), lambda bi, i: (bi, 0, i)),
                  pl.BlockSpec((1, s, D_PAD), lambda bi, i: (bi, 0, 0)),
                  pl.BlockSpec((1, d, s), lambda bi, i: (bi, 0, 0))],
        out_specs=pl.BlockSpec((1, blk, d), lambda bi, i: (bi, i, 0)),
        out_shape=jax.ShapeDtypeStruct((b, s, d), BF16),
        compiler_params=_compiler_params("parallel", "arbitrary"),
        name="prompt_attention",
    )(qt, k_aug, vt)


def _decode_body(pt_ref, q_ref, kn_ref, vn_ref, fn_ref, ck_hbm, cv_hbm, cf_hbm, o_ref,
                 kbuf, vbuf, fbuf, sem, *, layer, n_pages, page):
    b = pl.program_id(0)
    nb = pl.num_programs(0)
    slot = lax.rem(b, 2)
    past = n_pages * page

    def page_copies(seq, slot):
        copies = []
        for pg in range(n_pages):
            phys = pt_ref[seq * n_pages + pg]
            cols = pl.ds(pg * page, page)
            copies.append(pltpu.make_async_copy(ck_hbm.at[layer, phys], kbuf.at[slot, :, cols], sem.at[0, slot]))
            copies.append(pltpu.make_async_copy(cv_hbm.at[layer, phys], vbuf.at[slot, :, cols], sem.at[1, slot]))
            copies.append(pltpu.make_async_copy(cf_hbm.at[layer, phys], fbuf.at[slot, :, cols], sem.at[2, slot]))
        return copies

    @pl.when(b == 0)
    def _():
        for cp in page_copies(0, 0):
            cp.start()

    @pl.when(b + 1 < nb)
    def _():
        for cp in page_copies(b + 1, 1 - slot):
            cp.start()

    for cp in page_copies(b, slot):
        cp.wait()

    d = q_ref.shape[-1]
    head_of_lane = lax.shift_right_logical(lax.broadcasted_iota(jnp.int32, (N_HEADS, d), 1), 6)
    own = head_of_lane == lax.broadcasted_iota(jnp.int32, (N_HEADS, d), 0)
    qbd = jnp.where(own, q_ref[0].astype(F32), 0.0)
    s = jnp.dot(qbd.astype(BF16), kbuf[slot].astype(BF16), preferred_element_type=F32)

    f = fbuf[slot]
    lane = lax.broadcasted_iota(jnp.int32, f.shape, 1)
    x = jnp.where(lane < past - 1, pltpu.roll(f, past - 1, axis=1), 0.0)
    shift = 1
    while shift < past:
        x = x + jnp.where(lane < past - shift, pltpu.roll(x, past - shift, axis=1), 0.0)
        shift *= 2
    s = s + (x + fn_ref[0])

    s_new = jnp.sum(qbd * kn_ref[0], axis=1, keepdims=True)
    m = jnp.maximum(jnp.max(s, axis=1, keepdims=True), s_new)
    pr = jnp.exp(s - m)
    p_new = jnp.exp(s_new - m)
    denom = jnp.sum(pr, axis=1, keepdims=True) + p_new
    o8 = lax.dot_general(pr.astype(BF16), vbuf[slot].astype(BF16), (((1,), (1,)), ((), ())),
                         preferred_element_type=F32) + p_new * vn_ref[0]
    o_ref[0] = jnp.sum(jnp.where(own, o8 / denom, 0.0), axis=0, keepdims=True).astype(BF16)


def _decode_attention(q, k_new, v_new, f_new, cache_k, cache_v, cache_ft, page_table, layer):
    bs, d = q.shape
    n_pages = page_table.shape[1]
    page = cache_k.shape[3]
    past = n_pages * page
    per_seq = lambda shape: pl.BlockSpec((1,) + shape, lambda b, pt: (b, 0, 0))
    any_space = pl.BlockSpec(memory_space=pl.ANY)
    grid_spec = pltpu.PrefetchScalarGridSpec(
        num_scalar_prefetch=1,
        grid=(bs,),
        in_specs=[per_seq((1, d)), per_seq((1, d)), per_seq((1, d)), per_seq((N_HEADS, 1)),
                  any_space, any_space, any_space],
        out_specs=per_seq((1, d)),
        scratch_shapes=[pltpu.VMEM((2, d, past), F32), pltpu.VMEM((2, d, past), F32),
                        pltpu.VMEM((2, N_HEADS, past), F32), pltpu.SemaphoreType.DMA((3, 2))],
    )
    out = pl.pallas_call(
        functools.partial(_decode_body, layer=layer, n_pages=n_pages, page=page),
        grid_spec=grid_spec,
        out_shape=jax.ShapeDtypeStruct((bs, 1, d), BF16),
        compiler_params=_compiler_params("arbitrary"),
        name="decode_attention",
    )(page_table.reshape(-1), q.reshape(bs, 1, d), k_new.reshape(bs, 1, d), v_new.reshape(bs, 1, d),
      f_new.reshape(bs, N_HEADS, 1), cache_k, cache_v, cache_ft)
    return out.reshape(bs, d)


def _merge_tail(conv, o_ref, ga_ref, gb_ref, x_ref, lg_ref, lb_ref, wc_ref, wa_ref, wo_ref, out_ref):
    mu = jnp.mean(conv, axis=-1, keepdims=True)
    dev = conv - mu
    y = dev * lax.rsqrt(jnp.mean(dev * dev, axis=-1, keepdims=True) + EPS) * lg_ref[...] + lb_ref[...]
    a_out = jnp.dot((y * _sigmoid(y)).astype(BF16), wc_ref[...], preferred_element_type=F32)
    b_out = jnp.dot(o_ref[...], wa_ref[...], preferred_element_type=F32)
    mix = ga_ref[...].astype(F32) * a_out + gb_ref[...].astype(F32) * b_out
    out_ref[...] = x_ref[...] + jnp.dot(mix.astype(BF16), wo_ref[...], preferred_element_type=F32)


def _merge_prompt_body(u_ref, halo_ref, o_ref, ga_ref, gb_ref, x_ref, cw_ref, cb_ref, lg_ref, lb_ref,
                       wc_ref, wa_ref, wo_ref, out_ref, win_ref, conv_ref, *, tiles_per_seq, width):
    tm = u_ref.shape[0]
    first = lax.rem(pl.program_id(0), tiles_per_seq) == 0
    win_ref[0, 0:CONV_HALO, :] = jnp.where(first, 0.0, halo_ref[...])
    win_ref[0, CONV_HALO:, :] = u_ref[...]
    span = tm + CONV_HALO - SUBLANES
    for r in range(1, SUBLANES):
        win_ref[r, 0:span, :] = win_ref[0, r:r + span, :]
    lead = CONV_HALO - (width - 1)
    for r0 in range(0, tm, CONV_ROWS):
        acc = jnp.broadcast_to(cb_ref[...], (CONV_ROWS, cb_ref.shape[-1]))
        for j in range(width):
            r = (lead + j) % SUBLANES
            base = r0 + lead + j - r
            acc = acc + cw_ref[j:j + 1, :] * win_ref[r, base:base + CONV_ROWS, :]
        conv_ref[r0:r0 + CONV_ROWS, :] = acc
    _merge_tail(conv_ref[...], o_ref, ga_ref, gb_ref, x_ref, lg_ref, lb_ref, wc_ref, wa_ref, wo_ref, out_ref)


def _merge_sample_body(u_ref, st_ref, o_ref, ga_ref, gb_ref, x_ref, cw_ref, cb_ref, lg_ref, lb_ref,
                       wc_ref, wa_ref, wo_ref, out_ref, new_st_ref, *, width):
    u = u_ref[...]
    conv = cb_ref[...] + cw_ref[width - 1:width, :] * u
    for j in range(width - 1):
        conv = conv + cw_ref[j:j + 1, :] * st_ref[j]
    new_st_ref[0:width - 2] = st_ref[1:width - 1]
    new_st_ref[width - 2] = u
    _merge_tail(conv, o_ref, ga_ref, gb_ref, x_ref, lg_ref, lb_ref, wc_ref, wa_ref, wo_ref, out_ref)


def _merge(u, o, ga, gb, x, layer, p, tm, tiles_per_seq=None, state=None):
    t, d_model = x.shape
    d_conv = u.shape[1]
    width = p["conv_a_w"].shape[1]
    row = lambda w: pl.BlockSpec((tm, w), lambda i: (i, 0))
    weights = [_resident((width, d_conv), layer), _resident((1, d_conv), layer), _resident((1, d_conv), layer),
               _resident((1, d_conv), layer), _resident((d_conv, d_model), layer),
               _resident((D_ATTN, d_model), layer), _resident((d_model, d_model), layer)]
    weight_args = (p["conv_a_w"], p["conv_a_b"], p["ln_a_g"], p["ln_a_b"], p["w_conv_out"], p["w_attn_out"], p["w_o"])
    rows = [row(D_ATTN), row(d_model), row(d_model), row(d_model)]
    if state is None:
        halo_blocks = tm // CONV_HALO
        halo = pl.BlockSpec((CONV_HALO, d_conv), lambda i: (jnp.maximum(i * halo_blocks - 1, 0), 0))
        body = functools.partial(_merge_prompt_body, tiles_per_seq=tiles_per_seq, width=width)
        in_specs = [row(d_conv), halo] + rows + weights
        args = (u, u, o, ga, gb, x) + weight_args
        scratch = [pltpu.VMEM((SUBLANES, tm + CONV_HALO, d_conv), F32), pltpu.VMEM((tm, d_conv), F32)]
        out_specs = [row(d_model)]
        out_shape = [jax.ShapeDtypeStruct((t, d_model), F32)]
    else:
        body = functools.partial(_merge_sample_body, width=width)
        state_spec = pl.BlockSpec((None, width - 1, tm, d_conv), lambda i: (layer, 0, i, 0))
        in_specs = [row(d_conv), state_spec] + rows + weights
        args = (u, state, o, ga, gb, x) + weight_args
        scratch = []
        out_specs = [row(d_model), pl.BlockSpec((width - 1, tm, d_conv), lambda i: (0, i, 0))]
        out_shape = [jax.ShapeDtypeStruct((t, d_model), F32), jax.ShapeDtypeStruct((width - 1, t, d_conv), F32)]
    return pl.pallas_call(
        body, grid=(t // tm,), in_specs=in_specs, out_specs=out_specs, out_shape=out_shape, scratch_shapes=scratch,
        compiler_params=_compiler_params("parallel"), name="merge",
    )(*args)


def _ff_chunks(d_ff):
    return [(c0, min(c0 + FF_CHUNK, d_ff)) for c0 in range(0, d_ff, FF_CHUNK)]


def _ffn_prompt_body(x_ref, halo_ref, g2_ref, wup_ref, fw_ref, fb_ref, wdn_ref, out_ref, gtail_ref, gext_ref,
                     hidden_ref, *, tiles_per_seq, d_ff):
    tm = x_ref.shape[0]
    first = lax.rem(pl.program_id(0), tiles_per_seq) == 0
    x = x_ref[...]
    xn = _rms_scale(x, g2_ref[...]).astype(BF16)
    xn_halo = _rms_scale(jnp.where(first, 0.0, halo_ref[...]), g2_ref[...]).astype(BF16)
    xn_ext = jnp.concatenate([xn_halo, xn], axis=0)
    for c0, c1 in _ff_chunks(d_ff):
        gext_ref[:, c0:c1] = jnp.dot(xn_ext, wup_ref[:, c0:c1], preferred_element_type=F32)
        up = jnp.dot(xn, wup_ref[:, d_ff + c0:d_ff + c1], preferred_element_type=F32)
        gc = fb_ref[:, c0:c1]
        for j in range(3):
            lo = FFN_HALO - 2 + j
            gc = gc + fw_ref[j:j + 1, c0:c1] * gext_ref[lo:lo + tm, c0:c1]
        hidden_ref[:, c0:c1] = (gc * _sigmoid(gc) * up).astype(BF16)
        gtail_ref[:, c0:c1] = gext_ref[FFN_HALO + tm - 8:FFN_HALO + tm, c0:c1]
    out_ref[...] = x + jnp.dot(hidden_ref[...], wdn_ref[...], preferred_element_type=F32)


def _ffn_sample_body(x_ref, st_ref, g2_ref, wup_ref, fw_ref, fb_ref, wdn_ref, out_ref, g_ref, *, d_ff):
    x = x_ref[...]
    xn = _rms_scale(x, g2_ref[...]).astype(BF16)
    acc = x
    for c0, c1 in _ff_chunks(d_ff):
        g = jnp.dot(xn, wup_ref[:, c0:c1], preferred_element_type=F32)
        up = jnp.dot(xn, wup_ref[:, d_ff + c0:d_ff + c1], preferred_element_type=F32)
        gc = (fb_ref[:, c0:c1] + fw_ref[0:1, c0:c1] * st_ref[0, :, c0:c1]
              + fw_ref[1:2, c0:c1] * st_ref[1, :, c0:c1] + fw_ref[2:3, c0:c1] * g)
        hidden = (gc * _sigmoid(gc) * up).astype(BF16)
        acc = acc + jnp.dot(hidden, wdn_ref[c0:c1, :], preferred_element_type=F32)
        g_ref[:, c0:c1] = g
    out_ref[...] = acc


def _ffn(x, layer, p, tm, tiles_per_seq=None, state=None):
    t, d_model = x.shape
    d_ff = p["w_down"].shape[1]
    row = pl.BlockSpec((tm, d_model), lambda i: (i, 0))
    weights = [_resident((1, d_model), layer), _resident((d_model, 2 * d_ff), layer), _resident((3, d_ff), layer),
               _resident((1, d_ff), layer), _resident((d_ff, d_model), layer)]
    weight_args = (p["norm2_g"], p["w_up"], p["conv_f_w"], p["conv_f_b"], p["w_down"])
    if state is None:
        halo_blocks = tm // FFN_HALO
        halo = pl.BlockSpec((FFN_HALO, d_model), lambda i: (jnp.maximum(i * halo_blocks - 1, 0), 0))
        body = functools.partial(_ffn_prompt_body, tiles_per_seq=tiles_per_seq, d_ff=d_ff)
        in_specs = [row, halo] + weights
        args = (x, x) + weight_args
        out_specs = [row, pl.BlockSpec((8, d_ff), lambda i: (i, 0))]
        out_shape = [jax.ShapeDtypeStruct((t, d_model), F32), jax.ShapeDtypeStruct((t // tm * 8, d_ff), F32)]
        scratch = [pltpu.VMEM((tm + FFN_HALO, d_ff), F32), pltpu.VMEM((tm, d_ff), BF16)]
    else:
        body = functools.partial(_ffn_sample_body, d_ff=d_ff)
        in_specs = [row, pl.BlockSpec((None, 2, tm, d_ff), lambda i: (layer, 0, i, 0))] + weights
        args = (x, state) + weight_args
        out_specs = [row, pl.BlockSpec((tm, d_ff), lambda i: (i, 0))]
        out_shape = [jax.ShapeDtypeStruct((t, d_model), F32), jax.ShapeDtypeStruct((t, d_ff), F32)]
        scratch = []
    return pl.pallas_call(
        body, grid=(t // tm,), in_specs=in_specs, out_specs=out_specs, out_shape=out_shape, scratch_shapes=scratch,
        compiler_params=_compiler_params("parallel"), name="conv_ffn",
    )(*args)


def kernel(x_prompt, x_sample, cache_k, cache_v, cache_logf, state_conv_a, state_conv_ffn, page_table, norm1_g, w_in, b_f, q_norm_g, k_norm_g, conv_a_w, conv_a_b, ln_a_g, ln_a_b, w_conv_out, w_attn_out, w_o, norm2_g, w_up, conv_f_w, conv_f_b, w_down):
    bp, sp, d_model = x_prompt.shape
    bs, ts, _ = x_sample.shape
    depth, n_phys, page, n_heads, head_dim = cache_k.shape
    d_conv = conv_a_w.shape[-1]
    d_ff = w_down.shape[1]
    assert (n_heads, head_dim, ts) == (N_HEADS, HEAD_DIM, 1)
    assert sp % ROW_TILE == 0 and sp % ATTN_BLOCK == 0 and conv_a_w.shape[1] - 1 <= CONV_HALO

    qkv_end = 2 * d_conv + 3 * D_ATTN
    lane_head = jnp.arange(D_ATTN) // HEAD_DIM
    row3 = lambda a: a[:, None, :]
    p = dict(
        d_conv=d_conv, w_in_a=w_in[..., :qkv_end].astype(BF16), w_in_g=w_in[..., qkv_end + N_HEADS:].astype(BF16),
        w_in_f=jnp.pad(w_in[..., qkv_end:qkv_end + N_HEADS], ((0, 0), (0, 0), (0, LANES - N_HEADS))).astype(BF16),
        norm1_g=row3(norm1_g),
        b_f=row3(jnp.pad(b_f, ((0, 0), (0, LANES - N_HEADS)))),
        q_norm_g=row3(jnp.tile(q_norm_g, (1, N_HEADS))), k_norm_g=row3(jnp.tile(k_norm_g, (1, N_HEADS))),
        seg=(lane_head[:, None] == lane_head[None, :]).astype(BF16),
        conv_a_w=conv_a_w, conv_a_b=row3(conv_a_b), ln_a_g=row3(ln_a_g), ln_a_b=row3(ln_a_b),
        w_conv_out=w_conv_out.astype(BF16), w_attn_out=w_attn_out.astype(BF16), w_o=w_o.astype(BF16),
        norm2_g=row3(norm2_g), w_up=w_up.astype(BF16), conv_f_w=conv_f_w, conv_f_b=row3(conv_f_b),
        w_down=w_down.astype(BF16),
    )
    cache_k2 = jnp.transpose(cache_k, (0, 1, 3, 4, 2)).reshape(depth, n_phys, D_ATTN, page)
    cache_v2 = jnp.transpose(cache_v, (0, 1, 3, 4, 2)).reshape(depth, n_phys, D_ATTN, page)
    cache_ft = jnp.swapaxes(cache_logf, 2, 3)
    state_a = jnp.swapaxes(state_conv_a, 1, 2)
    state_f = jnp.swapaxes(state_conv_ffn, 1, 2)

    tiles_per_seq = sp // ROW_TILE
    xp = x_prompt.reshape(bp * sp, d_model)
    xs = x_sample.reshape(bs, d_model)
    outs = [[] for _ in range(10)]
    kt_all = jnp.zeros((depth, bp, D_ATTN, sp), F32)
    vt_all = jnp.zeros((depth, bp, D_ATTN, sp), F32)
    for layer in range(depth):
        u, qt, kt_all, k_aug, vt_all, vtb, logf, ga, gb = _in_proj(
            xp, layer, p, ROW_TILE, tiles_per_seq=tiles_per_seq, kv_all=(kt_all, vt_all))
        o = _prompt_attention(qt, k_aug.reshape(bp, sp, D_PAD), vtb)
        x1, = _merge(u, o.reshape(bp * sp, D_ATTN), ga, gb, xp, layer, p, ROW_TILE, tiles_per_seq=tiles_per_seq)
        xp, gtail = _ffn(x1, layer, p, ROW_TILE, tiles_per_seq=tiles_per_seq)
        outs[2].append(logf.reshape(bp, sp, N_HEADS))
        outs[3].append(u.reshape(bp, sp, d_conv)[:, sp - (conv_a_w.shape[1] - 1):])
        outs[4].append(gtail.reshape(bp, tiles_per_seq, 8, d_ff)[:, -1, 6:])

        u, q, k, kb, v, vb, logf, ga, gb = _in_proj(xs, layer, p, bs)
        o = _decode_attention(q, k, v, logf, cache_k2, cache_v2, cache_ft, page_table, layer)
        x1, new_state_a = _merge(u, o, ga, gb, xs, layer, p, bs, state=state_a)
        xs, g_new = _ffn(x1, layer, p, bs, state=state_f)
        outs[5].append(k.reshape(bs, ts, N_HEADS, HEAD_DIM))
        outs[6].append(v.reshape(bs, ts, N_HEADS, HEAD_DIM))
        outs[7].append(logf.reshape(bs, ts, N_HEADS))
        outs[8].append(new_state_a)
        outs[9].append(jnp.concatenate([state_conv_ffn[layer][:, 1:], g_new[:, None, :]], axis=1))

    outs = [kt_all, vt_all] + [jnp.stack(o) for o in outs[2:]]
    for idx in (0, 1):
        outs[idx] = jnp.transpose(outs[idx].reshape(depth, bp, N_HEADS, HEAD_DIM, sp), (0, 1, 4, 2, 3))
    outs[8] = jnp.swapaxes(outs[8], 1, 2)
    return (xp.reshape(bp, sp, d_model), xs.reshape(bs, ts, d_model)) + tuple(outs)
```
